```python
import math
import jax, jax.numpy as jnp
from jax import lax
import numpy as np

D_MODEL = 2048
BATCH = 1
SEQ = 8192
DEPTH = 4

HEAD_DIM = 64
S5_CH = D_MODEL // 4
S5_GROUP = 16
S5_NG = S5_CH // S5_GROUP
S5_P = 64
SB_HEADS = D_MODEL // 4 // HEAD_DIM
SB_W = SB_HEADS * HEAD_DIM
NSA_HEADS = D_MODEL // 2 // HEAD_DIM
NSA_KV = 4
NSA_REP = NSA_HEADS // NSA_KV
NSA_W = NSA_HEADS * HEAD_DIM
NSA_KV_W = NSA_KV * HEAD_DIM
CMP_LEN = 32
CMP_STRIDE = 16
CMP_HID = 128
SEL_BLOCK = 64
SEL_TOPN = 16
WINDOW = 512
Q_BLOCK = 128
FORCE_SCORE = 1e4
N_BUCKETS = 32
MAX_DIST = 1024
N_MEM = 256
XA_HEADS = 4
XA_HEAD_DIM = 128
XA_W = XA_HEADS * XA_HEAD_DIM
D_FF = 256 * ((8 * D_MODEL // 3 + 255) // 256)
CONV_W = 3
ALPHA = (2.0 * DEPTH) ** 0.25
BETA = (8.0 * DEPTH) ** -0.25
LN_EPS = 1e-5
IN_SIZES = (S5_CH, SB_W, SB_W, SB_W, NSA_W,
            NSA_KV_W, NSA_KV_W, NSA_KV_W, NSA_KV_W, NSA_KV_W, NSA_KV_W, 3 * NSA_HEADS)
N_IN = sum(IN_SIZES)
D_MIX = S5_CH + SB_W + NSA_W

kernel_name = "hybrid_s5_stickbreak_nsa_trunk"


def layer_norm(x, g, b):
    xf = x.astype(jnp.float32)
    mu = jnp.mean(xf, axis=-1, keepdims=True)
    var = jnp.mean(jnp.square(xf - mu), axis=-1, keepdims=True)
    return ((xf - mu) * lax.rsqrt(var + LN_EPS) * g + b).astype(x.dtype)


def masked_softmax(logits, mask):
    l = jnp.where(mask, logits.astype(jnp.float32), -1e30)
    m = jnp.max(l, axis=-1, keepdims=True)
    e = jnp.where(mask, jnp.exp(l - m), 0.0)
    return e / jnp.maximum(jnp.sum(e, axis=-1, keepdims=True), 1e-30)


def t5_bucket(dist):
    n = jnp.maximum(dist, 0)
    max_exact = N_BUCKETS // 2
    nf = jnp.maximum(n, 1).astype(jnp.float32)
    large = max_exact + (jnp.log(nf / max_exact) / math.log(MAX_DIST / max_exact)
                         * (N_BUCKETS - max_exact)).astype(jnp.int32)
    large = jnp.minimum(large, N_BUCKETS - 1)
    return jnp.where(n < max_exact, n, large)


def _complex_affine(e1, e2):
    a1r, a1i, b1r, b1i = e1
    a2r, a2i, b2r, b2i = e2
    return (a2r * a1r - a2i * a1i,
            a2r * a1i + a2i * a1r,
            a2r * b1r - a2i * b1i + b2r,
            a2r * b1i + a2i * b1r + b2i)


def s5_mixer(u, lam_re, lam_im, log_dt, b_re, b_im, c_re, c_im, d, w_glu, b_glu):
    bsz, t_len, _ = u.shape
    uf = u.astype(jnp.float32)
    ug = uf.reshape(bsz, t_len, S5_NG, S5_GROUP)
    lr = lam_re.astype(jnp.float32)
    li = lam_im.astype(jnp.float32)
    delta = jnp.exp(log_dt.astype(jnp.float32))[:, None]
    mag = jnp.exp(lr * delta)
    ar = mag * jnp.cos(li * delta)
    ai = mag * jnp.sin(li * delta)
    den = lr * lr + li * li
    cr = ((ar - 1.0) * lr + ai * li) / den
    ci = (ai * lr - (ar - 1.0) * li) / den
    br = cr[..., None] * b_re - ci[..., None] * b_im
    bi = cr[..., None] * b_im + ci[..., None] * b_re
    xr_in = jnp.einsum('btgc,gpc->btgp', ug, br)
    xi_in = jnp.einsum('btgc,gpc->btgp', ug, bi)
    a_r = jnp.broadcast_to(ar, xr_in.shape)
    a_i = jnp.broadcast_to(ai, xr_in.shape)
    _, _, sr, si = lax.associative_scan(_complex_affine, (a_r, a_i, xr_in, xi_in), axis=1)
    y = jnp.einsum('btgp,gcp->btgc', sr, c_re) - jnp.einsum('btgp,gcp->btgc', si, c_im)
    y = y.reshape(bsz, t_len, S5_CH) + d * uf
    z = jax.nn.gelu(y)
    out = z * jax.nn.sigmoid(z @ w_glu + b_glu)
    return out.astype(u.dtype)


def stick_breaking(q, k, v):
    bsz, t_len, n_h, hd = q.shape
    scale = hd ** -0.5
    kpos = jnp.arange(t_len)

    def block(q0):
        qb = lax.dynamic_slice_in_dim(q, q0, Q_BLOCK, axis=1)
        z = jnp.einsum('bqhd,bkhd->bhqk', qb, k).astype(jnp.float32) * scale
        t = q0 + jnp.arange(Q_BLOCK)
        mask = kpos[None, :] < t[:, None]
        log1m = jnp.where(mask, jax.nn.log_sigmoid(-z), 0.0)
        later = lax.cumsum(log1m, axis=3, reverse=True) - log1m
        w = jnp.where(mask, jnp.exp(jax.nn.log_sigmoid(z) + later), 0.0)
        o = jnp.einsum('bhqk,bkhd->bqhd', w.astype(v.dtype), v)
        return o.reshape(bsz, Q_BLOCK, n_h * hd)

    starts = jnp.arange(t_len // Q_BLOCK) * Q_BLOCK
    o = lax.map(block, starts)
    return o.transpose(1, 0, 2, 3).reshape(bsz, t_len, n_h * hd)


def nsa_mixer(q, k_cmp, v_cmp, k_slc, v_slc, k_swa, v_swa, gates,
              cmp_pos, cmp_w1, cmp_w2, rel_bias):
    bsz, t_len, n_h, hd = q.shape
    n_g, n_r = NSA_KV, NSA_REP
    scale = hd ** -0.5
    n_cmp = (t_len - CMP_LEN) // CMP_STRIDE + 1
    n_sel = t_len // SEL_BLOCK
    top_n = min(SEL_TOPN, n_sel)

    cmp_start = jnp.arange(n_cmp) * CMP_STRIDE
    idx = cmp_start[:, None] + jnp.arange(CMP_LEN)[None, :]

    def compress(kv, j):
        blk = kv[:, idx] + cmp_pos[j][None, None, :, None, :]
        blk = blk.transpose(0, 1, 3, 2, 4).reshape(bsz, n_cmp, n_g, CMP_LEN * hd)
        return jax.nn.gelu(blk @ cmp_w1[j]) @ cmp_w2[j]

    kc = compress(k_cmp, 0)
    vc = compress(v_cmp, 1)
    cmp_end = cmp_start + CMP_LEN - 1
    sel_ids = jnp.arange(n_sel)
    overlap = ((cmp_start[:, None] < (sel_ids[None, :] + 1) * SEL_BLOCK)
               & (cmp_end[:, None] >= sel_ids[None, :] * SEL_BLOCK)).astype(jnp.float32)

    kb = k_slc.reshape(bsz, n_sel, SEL_BLOCK, n_g, hd).transpose(0, 3, 1, 2, 4)
    vb = v_slc.reshape(bsz, n_sel, SEL_BLOCK, n_g, hd).transpose(0, 3, 1, 2, 4)
    pad = ((0, 0), (WINDOW, 0), (0, 0), (0, 0))
    kw = jnp.pad(k_swa, pad)
    vw = jnp.pad(v_swa, pad)

    tab = rel_bias.astype(jnp.float32)
    tab_g = tab.reshape(N_BUCKETS, n_g, n_r).transpose(1, 0, 2)
    b_idx = jnp.arange(bsz)[:, None, None, None]
    g_idx = jnp.arange(n_g)[None, :, None, None]

    def head_bias(dist):
        return tab[t5_bucket(dist)].reshape(*dist.shape, n_g, n_r).transpose(2, 3, 0, 1)

    def block(q0):
        t = q0 + jnp.arange(Q_BLOCK)
        qb = lax.dynamic_slice_in_dim(q, q0, Q_BLOCK, axis=1).reshape(bsz, Q_BLOCK, n_g, n_r, hd)
        dist_c = t[:, None] - cmp_end[None, :]
        s_c = jnp.einsum('bqgrd,bngd->bgrqn', qb, kc).astype(jnp.float32) * scale + head_bias(dist_c)
        p_c = masked_softmax(s_c, dist_c >= 0)
        o_c = jnp.einsum('bgrqn,bngd->bqgrd', p_c.astype(vc.dtype), vc)
        imp = jnp.einsum('bgrqn,nj->bgqj', p_c, overlap)
        cur = t // SEL_BLOCK
        forced = ((sel_ids[None, :] == 0) | (sel_ids[None, :] == cur[:, None])
                  | (sel_ids[None, :] == cur[:, None] - 1))
        valid = sel_ids[None, :] <= cur[:, None]
        score = jnp.where(valid, jnp.where(forced, FORCE_SCORE, imp), -1.0)
        _, sel = lax.top_k(score, top_n)
        ks = kb[b_idx, g_idx, sel].reshape(bsz, n_g, Q_BLOCK, top_n * SEL_BLOCK, hd)
        vs = vb[b_idx, g_idx, sel].reshape(bsz, n_g, Q_BLOCK, top_n * SEL_BLOCK, hd)
        pos_s = sel[..., None] * SEL_BLOCK + jnp.arange(SEL_BLOCK)
        dist_s = (t[:, None, None] - pos_s).reshape(bsz, n_g, Q_BLOCK, top_n * SEL_BLOCK)
        bias_s = tab_g[g_idx, t5_bucket(dist_s)].transpose(0, 1, 4, 2, 3)
        s_s = jnp.einsum('bqgrd,bgqmd->bgrqm', qb, ks).astype(jnp.float32) * scale + bias_s
        p_s = masked_softmax(s_s, (dist_s >= 0)[:, :, None])
        o_s = jnp.einsum('bgrqm,bgqmd->bqgrd', p_s.astype(vs.dtype), vs)
        kwb = lax.dynamic_slice_in_dim(kw, q0, WINDOW + Q_BLOCK, axis=1)
        vwb = lax.dynamic_slice_in_dim(vw, q0, WINDOW + Q_BLOCK, axis=1)
        pos_w = q0 - WINDOW + jnp.arange(WINDOW + Q_BLOCK)
        dist_w = t[:, None] - pos_w[None, :]
        mask_w = (dist_w >= 0) & (dist_w < WINDOW) & (pos_w[None, :] >= 0)
        s_w = jnp.einsum('bqgrd,bkgd->bgrqk', qb, kwb).astype(jnp.float32) * scale + head_bias(dist_w)
        p_w = masked_softmax(s_w, mask_w)
        o_w = jnp.einsum('bgrqk,bkgd->bqgrd', p_w.astype(vwb.dtype), vwb)
        gb = lax.dynamic_slice_in_dim(gates, q0, Q_BLOCK, axis=1).reshape(bsz, Q_BLOCK, n_g, n_r, 3)
        o = gb[..., 0:1] * o_c + gb[..., 1:2] * o_s + gb[..., 2:3] * o_w
        return o.reshape(bsz, Q_BLOCK, n_h * hd)

    starts = jnp.arange(t_len // Q_BLOCK) * Q_BLOCK
    o = lax.map(block, starts)
    return o.transpose(1, 0, 2, 3).reshape(bsz, t_len, n_h * hd)


def hybrid_mixer(h, w_in, w_out, lam_re, lam_im, log_dt, b_re, b_im, c_re, c_im, d,
                 w_glu, b_glu, cmp_pos, cmp_w1, cmp_w2, rel_bias):
    bsz, t_len, _ = h.shape
    proj = h @ w_in
    offsets = tuple(int(o) for o in np.cumsum(IN_SIZES)[:-1])
    (u5, sq, sk, sv, nq, kc, vc, ksl, vsl, ksw, vsw, g) = jnp.split(proj, offsets, axis=-1)

    def heads(a, n):
        return a.reshape(bsz, t_len, n, HEAD_DIM)

    y_s5 = s5_mixer(u5, lam_re, lam_im, log_dt, b_re, b_im, c_re, c_im, d, w_glu, b_glu)
    y_sb = stick_breaking(heads(sq, SB_HEADS), heads(sk, SB_HEADS), heads(sv, SB_HEADS))
    gates = jax.nn.sigmoid(g.reshape(bsz, t_len, NSA_HEADS, 3))
    y_nsa = nsa_mixer(heads(nq, NSA_HEADS), heads(kc, NSA_KV), heads(vc, NSA_KV),
                      heads(ksl, NSA_KV), heads(vsl, NSA_KV), heads(ksw, NSA_KV), heads(vsw, NSA_KV),
                      gates, cmp_pos, cmp_w1, cmp_w2, rel_bias)
    return jnp.concatenate([y_s5, y_sb, y_nsa], axis=-1) @ w_out


def memory_cross_attention(h, mem, wq, wkv, wo):
    bsz, t_len, _ = h.shape
    n_mem = mem.shape[1]
    q = (h @ wq).reshape(bsz, t_len, XA_HEADS, XA_HEAD_DIM)
    k, v = jnp.split(mem @ wkv, 2, axis=-1)
    k = k.reshape(bsz, n_mem, XA_HEADS, XA_HEAD_DIM)
    v = v.reshape(bsz, n_mem, XA_HEADS, XA_HEAD_DIM)
    s = jnp.einsum('bqhd,bmhd->bhqm', q, k).astype(jnp.float32) * (XA_HEAD_DIM ** -0.5)
    p = jax.nn.softmax(s, axis=-1)
    o = jnp.einsum('bhqm,bmhd->bqhd', p.astype(v.dtype), v).reshape(bsz, t_len, XA_W)
    return o @ wo


def conv_ffn(h, w_up, conv_w, conv_b, w_down):
    u = h @ w_up
    ch = u.shape[-1]
    u = lax.conv_general_dilated(u, conv_w[:, None, :], window_strides=(1,),
                                 padding=((CONV_W - 1, 0),),
                                 dimension_numbers=('NWC', 'WIO', 'NWC'),
                                 feature_group_count=ch) + conv_b
    a, g = jnp.split(u, 2, axis=-1)
    return (a * jax.nn.gelu(g)) @ w_down


def setup_inputs(seed: int = 0) -> dict:
    key = jax.random.key(seed)
    ks = iter(jax.random.split(key, 32))
    f32 = jnp.float32
    L = DEPTH

    def nrm(shape, scale):
        return scale * jax.random.normal(next(ks), shape, f32)

    lam_im0 = math.pi * jnp.arange(S5_P, dtype=f32)
    return {
        "x": nrm((BATCH, SEQ, D_MODEL), 1.0),
        "mem": nrm((BATCH, N_MEM, D_MODEL), 1.0),
        "w_in": nrm((L, D_MODEL, N_IN), D_MODEL ** -0.5),
        "w_out": nrm((L, D_MIX, D_MODEL), BETA * D_MIX ** -0.5),
        "s5_lambda_re": -0.5 + nrm((L, S5_NG, S5_P), 0.01),
        "s5_lambda_im": lam_im0 + nrm((L, S5_NG, S5_P), 0.01),
        "s5_log_dt": jax.random.uniform(next(ks), (L, S5_NG), f32, math.log(0.001), math.log(0.1)),
        "s5_b_re": nrm((L, S5_NG, S5_P, S5_GROUP), (2 * S5_GROUP) ** -0.5),
        "s5_b_im": nrm((L, S5_NG, S5_P, S5_GROUP), (2 * S5_GROUP) ** -0.5),
        "s5_c_re": nrm((L, S5_NG, S5_GROUP, S5_P), (2 * S5_P) ** -0.5),
        "s5_c_im": nrm((L, S5_NG, S5_GROUP, S5_P), (2 * S5_P) ** -0.5),
        "s5_d": nrm((L, S5_CH), 1.0),
        "s5_w_glu": nrm((L, S5_CH, S5_CH), S5_CH ** -0.5),
        "s5_b_glu": nrm((L, S5_CH), 0.01),
        "nsa_cmp_pos": nrm((L, 2, CMP_LEN, HEAD_DIM), 0.02),
        "nsa_cmp_w1": nrm((L, 2, CMP_LEN * HEAD_DIM, CMP_HID), (CMP_LEN * HEAD_DIM) ** -0.5),
        "nsa_cmp_w2": nrm((L, 2, CMP_HID, HEAD_DIM), CMP_HID ** -0.5),
        "rel_bias": nrm((N_BUCKETS, NSA_HEADS), 0.1),
        "xa_wq": nrm((L, D_MODEL, XA_W), D_MODEL ** -0.5),
        "xa_wkv": nrm((L, D_MODEL, 2 * XA_W), D_MODEL ** -0.5),
        "xa_wo": nrm((L, XA_W, D_MODEL), BETA * XA_W ** -0.5),
        "ffn_w_up": nrm((L, D_MODEL, 2 * D_FF), D_MODEL ** -0.5),
        "ffn_conv_w": nrm((L, CONV_W, 2 * D_FF), CONV_W ** -0.5),
        "ffn_conv_b": nrm((L, 2 * D_FF), 0.01),
        "ffn_w_down": nrm((L, D_FF, D_MODEL), BETA * D_FF ** -0.5),
        "ln_g": 1.0 + nrm((L, 3, D_MODEL), 0.01),
        "ln_b": nrm((L, 3, D_MODEL), 0.01),
    }


def reference(x, mem, w_in, w_out, s5_lambda_re, s5_lambda_im, s5_log_dt, s5_b_re, s5_b_im,
              s5_c_re, s5_c_im, s5_d, s5_w_glu, s5_b_glu, nsa_cmp_pos, nsa_cmp_w1, nsa_cmp_w2,
              rel_bias, xa_wq, xa_wkv, xa_wo, ffn_w_up, ffn_conv_w, ffn_conv_b, ffn_w_down,
              ln_g, ln_b):
    h = x
    for l in range(DEPTH):
        mix = hybrid_mixer(h, w_in[l], w_out[l], s5_lambda_re[l], s5_lambda_im[l], s5_log_dt[l],
                           s5_b_re[l], s5_b_im[l], s5_c_re[l], s5_c_im[l], s5_d[l],
                           s5_w_glu[l], s5_b_glu[l], nsa_cmp_pos[l], nsa_cmp_w1[l], nsa_cmp_w2[l],
                           rel_bias)
        h = layer_norm(ALPHA * h + mix, ln_g[l, 0], ln_b[l, 0])
        xa = memory_cross_attention(h, mem, xa_wq[l], xa_wkv[l], xa_wo[l])
        h = layer_norm(ALPHA * h + xa, ln_g[l, 1], ln_b[l, 1])
        ff = conv_ffn(h, ffn_w_up[l], ffn_conv_w[l], ffn_conv_b[l], ffn_w_down[l])
        h = layer_norm(ALPHA * h + ff, ln_g[l, 2], ln_b[l, 2])
    return h
```

```python
import functools
import math

import numpy as np
import jax
import jax.numpy as jnp
from jax import lax
from jax.experimental import pallas as pl
from jax.experimental.pallas import tpu as pltpu

F32 = jnp.float32
BF16 = jnp.bfloat16

D_MODEL = 2048
DEPTH = 4
HEAD_DIM = 64
S5_CH = 512
S5_GROUP = 16
S5_NG = 32
S5_P = 64
SB_HEADS = 8
NSA_HEADS = 16
NSA_KV = 4
NSA_REP = 4
CMP_LEN = 32
CMP_STRIDE = 16
CMP_HID = 128
SEL_BLOCK = 64
SEL_TOPN = 16
WINDOW = 512
FORCE_SCORE = 1e4
N_BUCKETS = 32
MAX_DIST = 1024
N_MEM = 256
XA_HEADS = 4
XA_HEAD_DIM = 128
XA_W = 512
D_FF = 5632
ALPHA = (2.0 * DEPTH) ** 0.25
LN_EPS = 1e-5
NEG = -1e30

LANE = 128
SUBLANE = 8
MXU_DIM = 256
VMEM_LIMIT = 56 * 1024 * 1024
TQ = 256
TK = 256
S5_SEG = 64
S5_CHUNK = SUBLANE * S5_SEG
S5_CB = 8
BIAS_SUB = 128
N_BIAS_SUB = 10


def _cparams(sem):
    return pltpu.CompilerParams(dimension_semantics=sem, vmem_limit_bytes=VMEM_LIMIT)


def _dot(a, b):
    return jnp.dot(a, b, preferred_element_type=F32)


def _dot_nt(a, b):
    return lax.dot_general(a, b, (((1,), (1,)), ((), ())), preferred_element_type=F32)


def _split_bf16(x):
    hi = x.astype(BF16)
    lo = (x - hi.astype(F32)).astype(BF16)
    return hi, lo


def _gelu(x):
    c = math.sqrt(2.0 / math.pi)
    return 0.5 * x * (1.0 + jnp.tanh(c * (x + 0.044715 * (x * x * x))))


def _sigmoid(x):
    return 1.0 / (1.0 + jnp.exp(-x))


def _layer_norm(x, g, b):
    mu = jnp.mean(x, axis=-1, keepdims=True)
    xc = x - mu
    var = jnp.mean(xc * xc, axis=-1, keepdims=True)
    return xc * lax.rsqrt(var + LN_EPS) * g + b


def _mm_kernel(a_ref, b_ref, o_ref):
    o_ref[...] = _dot(a_ref[...], b_ref[...]).astype(o_ref.dtype)


def _mm(a, b, out_dtype, tm, tn):
    m, k = a.shape
    n = b.shape[1]
    tm = min(tm, m)
    return pl.pallas_call(
        _mm_kernel,
        grid=(n // tn, m // tm),
        in_specs=[pl.BlockSpec((tm, k), lambda j, i: (i, 0)),
                  pl.BlockSpec((k, tn), lambda j, i: (0, j))],
        out_specs=pl.BlockSpec((tm, tn), lambda j, i: (i, j)),
        out_shape=jax.ShapeDtypeStruct((m, n), out_dtype),
        compiler_params=_cparams(("parallel", "arbitrary")),
        name="mm",
    )(a, b)


def _mm_heads_kernel(a_ref, b_ref, o_ref):
    o = _dot(a_ref[...], b_ref[...]).astype(o_ref.dtype)
    for r in range(o_ref.shape[0]):
        o_ref[r] = o[:, r * HEAD_DIM:(r + 1) * HEAD_DIM]


def _mm_heads(a, b, tm):
    m, k = a.shape
    n = b.shape[1]
    tn = MXU_DIM
    hpt = tn // HEAD_DIM
    tm = min(tm, m)
    return pl.pallas_call(
        _mm_heads_kernel,
        grid=(n // tn, m // tm),
        in_specs=[pl.BlockSpec((tm, k), lambda j, i: (i, 0)),
                  pl.BlockSpec((k, tn), lambda j, i: (0, j))],
        out_specs=pl.BlockSpec((hpt, tm, HEAD_DIM), lambda j, i: (j, i, 0)),
        out_shape=jax.ShapeDtypeStruct((n // HEAD_DIM, m, HEAD_DIM), BF16),
        compiler_params=_cparams(("parallel", "arbitrary")),
        name="mm_heads",
    )(a, b)


def _mm_ln_kernel(a_ref, w_ref, h_ref, g_ref, b_ref, of_ref, ob_ref, acc_ref):
    kk = pl.program_id(1)

    @pl.when(kk == 0)
    def _():
        acc_ref[...] = jnp.zeros_like(acc_ref)

    acc_ref[...] += _dot(a_ref[...], w_ref[...])

    @pl.when(kk == pl.num_programs(1) - 1)
    def _():
        y = _layer_norm(ALPHA * h_ref[...] + acc_ref[...], g_ref[...], b_ref[...])
        of_ref[...] = y
        ob_ref[...] = y.astype(BF16)


def _mm_ln(a, w, h, g, b, tm, tk):
    m, k = a.shape
    n = w.shape[1]
    tm = min(tm, m)
    return pl.pallas_call(
        _mm_ln_kernel,
        grid=(m // tm, k // tk),
        in_specs=[pl.BlockSpec((tm, tk), lambda i, kk: (i, kk)),
                  pl.BlockSpec((tk, n), lambda i, kk: (kk, 0)),
                  pl.BlockSpec((tm, n), lambda i, kk: (i, 0)),
                  pl.BlockSpec((1, n), lambda i, kk: (0, 0)),
                  pl.BlockSpec((1, n), lambda i, kk: (0, 0))],
        out_specs=[pl.BlockSpec((tm, n), lambda i, kk: (i, 0)),
                   pl.BlockSpec((tm, n), lambda i, kk: (i, 0))],
        out_shape=[jax.ShapeDtypeStruct((m, n), F32), jax.ShapeDtypeStruct((m, n), BF16)],
        scratch_shapes=[pltpu.VMEM((tm, n), F32)],
        compiler_params=_cparams(("parallel", "arbitrary")),
        name="mm_ln",
    )(a, w, h, g.reshape(1, n), b.reshape(1, n))


def _s5_kernel(u_ref, bm_ref, cm_ref, a_ref, d_ref, z_ref, s_ref, p_ref, carry_ref):
    tc = pl.program_id(1)
    w = s_ref.shape[1] // 2
    n_step = p_ref.shape[0]
    ar1 = a_ref[0:1, :]
    ai1 = a_ref[1:2, :]

    @pl.when(tc == 0)
    def _():
        carry_ref[...] = jnp.zeros_like(carry_ref)
        p_ref[0:1, 0:w] = ar1
        p_ref[0:1, w:2 * w] = ai1

        def pw(t, c):
            pr, pi = c
            nr = pr * ar1 - pi * ai1
            ni = pr * ai1 + pi * ar1
            p_ref[pl.ds(t, 1), 0:w] = nr
            p_ref[pl.ds(t, 1), w:2 * w] = ni
            return nr, ni

        lax.fori_loop(1, n_step, pw, (ar1, ai1))

    u = u_ref[...]
    s_ref[...] = _dot(u.astype(BF16), bm_ref[0])

    ar = jnp.broadcast_to(ar1, (SUBLANE, w))
    ai = jnp.broadcast_to(ai1, (SUBLANE, w))

    def step(t, c):
        sr, si = c
        r0 = pl.multiple_of(t * SUBLANE, SUBLANE)
        xr = s_ref[pl.ds(r0, SUBLANE), 0:w]
        xi = s_ref[pl.ds(r0, SUBLANE), w:2 * w]
        nr = ar * sr - ai * si + xr
        ni = ar * si + ai * sr + xi
        s_ref[pl.ds(r0, SUBLANE), 0:w] = nr
        s_ref[pl.ds(r0, SUBLANE), w:2 * w] = ni
        return nr, ni

    zero = jnp.zeros((SUBLANE, w), F32)
    fr, fi = lax.fori_loop(0, n_step, step, (zero, zero), unroll=8)

    amr = p_ref[n_step - 1:n_step, 0:w]
    ami = p_ref[n_step - 1:n_step, w:2 * w]
    row = lax.broadcasted_iota(jnp.int32, (SUBLANE, w), 0)
    cr = carry_ref[0:1, 0:w]
    ci = carry_ref[0:1, w:2 * w]
    cin_r = jnp.zeros((SUBLANE, w), F32)
    cin_i = jnp.zeros((SUBLANE, w), F32)
    for seg in range(SUBLANE):
        cin_r = jnp.where(row == seg, cr, cin_r)
        cin_i = jnp.where(row == seg, ci, cin_i)
        nr = fr[seg:seg + 1, :] + amr * cr - ami * ci
        ni = fi[seg:seg + 1, :] + amr * ci + ami * cr
        cr, ci = nr, ni
    carry_ref[0:1, 0:w] = cr
    carry_ref[0:1, w:2 * w] = ci

    def fix(t, c):
        r0 = pl.multiple_of(t * SUBLANE, SUBLANE)
        pr = p_ref[pl.ds(t, 1), 0:w]
        pi = p_ref[pl.ds(t, 1), w:2 * w]
        s_ref[pl.ds(r0, SUBLANE), 0:w] += pr * cin_r - pi * cin_i
        s_ref[pl.ds(r0, SUBLANE), w:2 * w] += pr * cin_i + pi * cin_r
        return c

    lax.fori_loop(0, n_step, fix, 0, unroll=8)

    y = _dot(s_ref[...].astype(BF16), cm_ref[0]) + d_ref[...] * u
    z_ref[...] = _gelu(y)


def _s5_glu_kernel(z_ref, w_ref, b_ref, o_ref):
    z = z_ref[...]
    gate = _sigmoid(_dot(z.astype(BF16), w_ref[...]) + b_ref[...])
    o_ref[...] = (z * gate).astype(o_ref.dtype)


def _s5_params(lam_re, lam_im, log_dt, b_re, b_im, c_re, c_im):
    delta = jnp.exp(log_dt)[:, None]
    mag = jnp.exp(lam_re * delta)
    ar = mag * jnp.cos(lam_im * delta)
    ai = mag * jnp.sin(lam_im * delta)
    den = lam_re * lam_re + lam_im * lam_im
    cr = ((ar - 1.0) * lam_re + ai * lam_im) / den
    ci = (ai * lam_re - (ar - 1.0) * lam_im) / den
    br = cr[..., None] * b_re - ci[..., None] * b_im
    bi = cr[..., None] * b_im + ci[..., None] * b_re
    ncb = S5_NG // S5_CB
    eye = jnp.eye(S5_CB, dtype=F32)

    def in_mat(b):
        b = b.reshape(ncb, S5_CB, S5_P, S5_GROUP)
        m = jnp.einsum('ngpc,gh->ngchp', b, eye)
        return m.reshape(ncb, S5_CB * S5_GROUP, S5_CB * S5_P)

    def out_mat(c):
        c = c.reshape(ncb, S5_CB, S5_GROUP, S5_P)
        m = jnp.einsum('ngcp,gh->ngphc', c, eye)
        return m.reshape(ncb, S5_CB * S5_P, S5_CB * S5_GROUP)

    bm = jnp.concatenate([in_mat(br), in_mat(bi)], axis=2).astype(BF16)
    cm = jnp.concatenate([out_mat(c_re), -out_mat(c_im)], axis=1).astype(BF16)
    a = jnp.stack([ar.reshape(-1), ai.reshape(-1)], axis=0)
    return bm, cm, a


def _s5(u, p, w_glu, b_glu, d):
    t_len = u.shape[0]
    nc = t_len // S5_CHUNK
    ncb = S5_NG // S5_CB
    wl = S5_CB * S5_GROUP
    ws = S5_CB * S5_P
    bm, cm, a = p
    up = u.reshape(nc, SUBLANE, S5_SEG, S5_CH).transpose(0, 2, 1, 3).reshape(t_len, S5_CH)
    z = pl.pallas_call(
        _s5_kernel,
        grid=(ncb, nc),
        in_specs=[pl.BlockSpec((S5_CHUNK, wl), lambda c, t: (t, c)),
                  pl.BlockSpec((1, wl, 2 * ws), lambda c, t: (c, 0, 0)),
                  pl.BlockSpec((1, 2 * ws, wl), lambda c, t: (c, 0, 0)),
                  pl.BlockSpec((2, ws), lambda c, t: (0, c)),
                  pl.BlockSpec((1, wl), lambda c, t: (0, c))],
        out_specs=pl.BlockSpec((S5_CHUNK, wl), lambda c, t: (t, c)),
        out_shape=jax.ShapeDtypeStruct((t_len, S5_CH), F32),
        scratch_shapes=[pltpu.VMEM((S5_CHUNK, 2 * ws), F32),
                        pltpu.VMEM((S5_SEG, 2 * ws), F32),
                        pltpu.VMEM((SUBLANE, 2 * ws), F32)],
        compiler_params=_cparams(("parallel", "arbitrary")),
        name="s5_scan",
    )(up, bm, cm, a, d.reshape(1, S5_CH))
    tm = min(1024, t_len)
    y = pl.pallas_call(
        _s5_glu_kernel,
        grid=(t_len // tm,),
        in_specs=[pl.BlockSpec((tm, S5_CH), lambda i: (i, 0)),
                  pl.BlockSpec((S5_CH, S5_CH), lambda i: (0, 0)),
                  pl.BlockSpec((1, S5_CH), lambda i: (0, 0))],
        out_specs=pl.BlockSpec((tm, S5_CH), lambda i: (i, 0)),
        out_shape=jax.ShapeDtypeStruct((t_len, S5_CH), BF16),
        compiler_params=_cparams(("parallel",)),
        name="s5_glu",
    )(z, w_glu.astype(BF16), b_glu.reshape(1, S5_CH))
    return y.reshape(nc, S5_SEG, SUBLANE, S5_CH).transpose(0, 2, 1, 3).reshape(t_len, S5_CH)


def _sb_kernel(q_ref, k_ref, v_ref, u_ref, o_ref):
    qi = pl.program_id(1)
    q = q_ref[0] * 0.125
    umat = u_ref[...]

    def tile(kt, r_sum, acc, diag):
        k0 = pl.multiple_of(kt * TK, TK)
        k = k_ref[0, pl.ds(k0, TK), :]
        v = v_ref[0, pl.ds(k0, TK), :]
        z = _dot_nt(q, k)
        l1m = -(jnp.maximum(z, 0.0) + jnp.log(1.0 + jnp.exp(-jnp.abs(z))))
        lsz = l1m + z
        if diag:
            row = lax.broadcasted_iota(jnp.int32, (TQ, TK), 0)
            col = lax.broadcasted_iota(jnp.int32, (TQ, TK), 1)
            mask = col < row
            l1m = jnp.where(mask, l1m, 0.0)
        hi, lo = _split_bf16(l1m)
        later = _dot(hi, umat) + _dot(lo, umat) + r_sum
        w = jnp.exp(lsz + later)
        if diag:
            w = jnp.where(mask, w, 0.0)
        acc = acc + _dot(w.astype(BF16), v)
        r_sum = r_sum + jnp.sum(l1m, axis=1, keepdims=True)
        return r_sum, acc

    r_sum, acc = tile(qi, jnp.zeros((TQ, 1), F32), jnp.zeros((TQ, HEAD_DIM), F32), True)

    def body(n, c):
        return tile(qi - 1 - n, c[0], c[1], False)

    r_sum, acc = lax.fori_loop(0, qi, body, (r_sum, acc))
    o_ref[0] = acc.astype(o_ref.dtype)


def _stick_breaking(proj, umat):
    t_len = proj.shape[1]
    return pl.pallas_call(
        _sb_kernel,
        grid=(SB_HEADS, t_len // TQ),
        in_specs=[pl.BlockSpec((1, TQ, HEAD_DIM), lambda h, i: (h, i, 0)),
                  pl.BlockSpec((1, t_len, HEAD_DIM), lambda h, i: (SB_HEADS + h, 0, 0)),
                  pl.BlockSpec((1, t_len, HEAD_DIM), lambda h, i: (2 * SB_HEADS + h, 0, 0)),
                  pl.BlockSpec((TK, TK), lambda h, i: (0, 0))],
        out_specs=pl.BlockSpec((1, TQ, HEAD_DIM), lambda h, i: (h, i, 0)),
        out_shape=jax.ShapeDtypeStruct((SB_HEADS, t_len, HEAD_DIM), BF16),
        compiler_params=_cparams(("parallel", "arbitrary")),
        name="stick_breaking",
    )(proj, proj, proj, umat)


def _t5_bucket(dist):
    n = jnp.maximum(dist, 0)
    max_exact = N_BUCKETS // 2
    nf = jnp.maximum(n, 1).astype(jnp.float32)
    large = max_exact + (jnp.log(nf / max_exact) / math.log(MAX_DIST / max_exact)
                         * (N_BUCKETS - max_exact)).astype(jnp.int32)
    large = jnp.minimum(large, N_BUCKETS - 1)
    return jnp.where(n < max_exact, n, large)


def _bias_lookup_kernel(tab_ref, idx_ref, o_ref):
    h = pl.program_id(0)
    idx = idx_ref[...]
    acc = jnp.zeros(idx.shape, F32)
    for b in range(N_BUCKETS):
        acc = jnp.where(idx == b, tab_ref[b, h], acc)
    o_ref[0] = acc


def _bias_lookup(tab, idx):
    r, c = idx.shape
    n_h = tab.shape[1]
    return pl.pallas_call(
        _bias_lookup_kernel,
        grid=(n_h,),
        in_specs=[pl.BlockSpec(memory_space=pltpu.SMEM),
                  pl.BlockSpec((r, c), lambda h: (0, 0))],
        out_specs=pl.BlockSpec((1, r, c), lambda h: (h, 0, 0)),
        out_shape=jax.ShapeDtypeStruct((n_h, r, c), F32),
        compiler_params=_cparams(("arbitrary",)),
        name="bias_lookup",
    )(tab, idx)


def _bias_tables(rel_bias):
    i = jnp.arange(BIAS_SUB)
    d_win = (BIAS_SUB * jnp.arange(N_BIAS_SUB)[:, None, None] + i[None, :, None] - i[None, None, :])
    idx_win = _t5_bucket(d_win).reshape(N_BIAS_SUB * BIAS_SUB, BIAS_SUB)
    win = _bias_lookup(rel_bias, idx_win).reshape(NSA_HEADS, N_BIAS_SUB, BIAS_SUB, BIAS_SUB)
    n_m = (LANE * CMP_STRIDE) // TQ
    m = jnp.arange(n_m)[:, None, None]
    qi = jnp.arange(TQ)[None, :, None]
    c = jnp.arange(2 * LANE)[None, None, :]
    d_cmp = TQ * m + qi + LANE * CMP_STRIDE - CMP_STRIDE * c - (CMP_LEN - 1)
    idx_cmp = _t5_bucket(d_cmp).reshape(n_m * TQ, 2 * LANE)
    cmp_t = _bias_lookup(rel_bias, idx_cmp).reshape(NSA_HEADS, n_m, TQ, 2 * LANE)
    return win, cmp_t


def _compress_kernel(c0_ref, c1_ref, w1_ref, pos_ref, w2_ref, o_ref):
    half = c0_ref.shape[3]
    w1 = w1_ref[0]
    posb = _dot(pos_ref[0], w1)[0:1, :]
    hid = _dot(c0_ref[0, 0], w1[0:half, :]) + _dot(c1_ref[0, 0], w1[half:, :]) + posb
    o_ref[0, 0] = _dot(_gelu(hid).astype(BF16), w2_ref[0]).astype(o_ref.dtype)


def _compress(proj, head0, cmp_pos, cmp_w1, cmp_w2):
    t_len = proj.shape[1]
    n16 = t_len // CMP_STRIDE
    half = CMP_STRIDE * HEAD_DIM
    kv = proj[head0:head0 + 2 * NSA_KV].reshape(2, NSA_KV, n16, half)
    kv_next = jnp.concatenate([kv[:, :, 1:], jnp.zeros((2, NSA_KV, 1, half), BF16)], axis=2)
    pos = jnp.broadcast_to(cmp_pos.reshape(2, 1, CMP_LEN * HEAD_DIM), (2, SUBLANE, CMP_LEN * HEAD_DIM))
    return pl.pallas_call(
        _compress_kernel,
        grid=(2, NSA_KV),
        in_specs=[pl.BlockSpec((1, 1, n16, half), lambda j, g: (j, g, 0, 0)),
                  pl.BlockSpec((1, 1, n16, half), lambda j, g: (j, g, 0, 0)),
                  pl.BlockSpec((1, 2 * half, CMP_HID), lambda j, g: (j, 0, 0)),
                  pl.BlockSpec((1, SUBLANE, 2 * half), lambda j, g: (j, 0, 0)),
                  pl.BlockSpec((1, CMP_HID, HEAD_DIM), lambda j, g: (j, 0, 0))],
        out_specs=pl.BlockSpec((1, 1, n16, HEAD_DIM), lambda j, g: (j, g, 0, 0)),
        out_shape=jax.ShapeDtypeStruct((2, NSA_KV, n16, HEAD_DIM), BF16),
        compiler_params=_cparams(("parallel", "arbitrary")),
        name="nsa_compress",
    )(kv, kv_next, cmp_w1.astype(BF16), pos.astype(BF16), cmp_w2.astype(BF16))


def _cmp_select_kernel(q_ref, kc_ref, vc_ref, bias_ref, tab_ref, ov_ref, oc_ref, mn_ref, *, n_cmp):
    g = pl.program_id(0)
    qi = pl.program_id(1)
    n16 = kc_ref.shape[2]
    n_ct = n16 // LANE
    kc = kc_ref[0, 0]
    vc = vc_ref[0, 0]
    q_per_ct = (LANE * CMP_STRIDE) // TQ
    jd = qi // q_per_ct
    t = qi * TQ + lax.broadcasted_iota(jnp.int32, (TQ, n16), 0)
    n = lax.broadcasted_iota(jnp.int32, (TQ, n16), 1)
    valid = (CMP_STRIDE * n + (CMP_LEN - 1) <= t) & (n < n_cmp)
    psum = jnp.zeros((TQ, n16), F32)
    for r in range(NSA_REP):
        q = q_ref[r] * 0.125
        far = tab_ref[N_BUCKETS - 1, g * NSA_REP + r]
        cols = []
        for jc in range(n_ct):
            near = jnp.where(jc == jd, bias_ref[r, 0, :, LANE:2 * LANE],
                             bias_ref[r, 0, :, 0:LANE])
            cols.append(jnp.where((jc == jd) | (jc == jd - 1), near, far))
        bias = cols[0] if n_ct == 1 else jnp.concatenate(cols, axis=1)
        s = jnp.where(valid, _dot_nt(q, kc) + bias, NEG)
        mx = jnp.max(s, axis=1, keepdims=True)
        e = jnp.where(valid, jnp.exp(s - mx), 0.0)
        p = e / jnp.maximum(jnp.sum(e, axis=1, keepdims=True), 1e-30)
        psum = psum + p
        oc_ref[:, r * HEAD_DIM:(r + 1) * HEAD_DIM] = _dot(p.astype(BF16), vc)
    hi, lo = _split_bf16(psum)
    ov = ov_ref[...]
    imp = _dot_nt(ov, hi) + _dot_nt(ov, lo)
    n_sel = imp.shape[0]
    j = lax.broadcasted_iota(jnp.int32, (n_sel, TQ), 0)
    tq = qi * TQ + lax.broadcasted_iota(jnp.int32, (n_sel, TQ), 1)
    cur = tq // SEL_BLOCK
    forced = (j == 0) | (j == cur) | (j == cur - 1)
    score = jnp.where(j <= cur, jnp.where(forced, FORCE_SCORE, imp), -1.0)
    jf = j.astype(F32)
    sel = jnp.zeros((n_sel, TQ), F32)
    for _ in range(min(SEL_TOPN, n_sel)):
        mx = jnp.max(score, axis=0, keepdims=True)
        first = jnp.min(jnp.where(score == mx, jf, float(n_sel)), axis=0, keepdims=True)
        hit = jf == first
        sel = jnp.where(hit, 1.0, sel)
        score = jnp.where(hit, -jnp.inf, score)
    mn_ref[0] = (sel - 1.0).astype(mn_ref.dtype)


def _cmp_select(proj, q_head0, kvc, bias_cmp, rel_bias, overlap_t):
    t_len = proj.shape[1]
    n16 = kvc.shape[2]
    n_sel = t_len // SEL_BLOCK
    n_cmp = (t_len - CMP_LEN) // CMP_STRIDE + 1
    q_per_ct = (LANE * CMP_STRIDE) // TQ
    hb = q_head0 // NSA_REP
    return pl.pallas_call(
        functools.partial(_cmp_select_kernel, n_cmp=n_cmp),
        grid=(NSA_KV, t_len // TQ),
        in_specs=[pl.BlockSpec((NSA_REP, TQ, HEAD_DIM), lambda g, i: (hb + g, i, 0)),
                  pl.BlockSpec((1, 1, n16, HEAD_DIM), lambda g, i: (0, g, 0, 0)),
                  pl.BlockSpec((1, 1, n16, HEAD_DIM), lambda g, i: (1, g, 0, 0)),
                  pl.BlockSpec((NSA_REP, 1, TQ, 2 * LANE), lambda g, i: (g, i % q_per_ct, 0, 0)),
                  pl.BlockSpec(memory_space=pltpu.SMEM),
                  pl.BlockSpec((n_sel, n16), lambda g, i: (0, 0))],
        out_specs=[pl.BlockSpec((TQ, NSA_REP * HEAD_DIM), lambda g, i: (i, g)),
                   pl.BlockSpec((1, n_sel, TQ), lambda g, i: (g, 0, i))],
        out_shape=[jax.ShapeDtypeStruct((t_len, NSA_HEADS * HEAD_DIM), F32),
                   jax.ShapeDtypeStruct((NSA_KV, n_sel, t_len), BF16)],
        compiler_params=_cparams(("parallel", "arbitrary")),
        name="nsa_cmp_select",
    )(proj, kvc, kvc, bias_cmp, rel_bias, overlap_t)


def _bias_tile(win_ref, r, delta):
    sub = TQ // BIAS_SUB
    rows = []
    for a in range(sub):
        cols = []
        for b in range(sub):
            idx = jnp.maximum(sub * delta + a - b, 0)
            cols.append(win_ref[r, idx])
        rows.append(jnp.concatenate(cols, axis=1))
    return jnp.concatenate(rows, axis=0)


def _slc_swa_kernel(q_ref, mn_ref, ka_ref, vs_ref, kw_ref, vw_ref, win_ref, tab_ref, gate_ref,
                    oc_ref, o_ref, qa_ref, m_ref, l_ref, acc_ref):
    g = pl.program_id(0)
    qi = pl.program_id(1)
    n_band = (N_BIAS_SUB * BIAS_SUB) // TK
    big = mn_ref[0] * jnp.asarray(1e30, BF16)
    for r in range(NSA_REP):
        qa_ref[r, :, 0:HEAD_DIM] = q_ref[r] * 0.125
        qa_ref[r, :, HEAD_DIM:LANE] = jnp.zeros((TQ, LANE - HEAD_DIM), BF16)
        qa_ref[r, :, LANE:2 * LANE] = big
    m_ref[...] = jnp.full_like(m_ref, NEG)
    l_ref[...] = jnp.zeros_like(l_ref)
    acc_ref[...] = jnp.zeros_like(acc_ref)
    row = lax.broadcasted_iota(jnp.int32, (TQ, TK), 0)
    col = lax.broadcasted_iota(jnp.int32, (TQ, TK), 1)

    def slc_tile(kt, mode):
        k0 = pl.multiple_of(kt * TK, TK)
        ka = ka_ref[0, pl.ds(k0, TK), :]
        v = vs_ref[0, pl.ds(k0, TK), :]
        for r in range(NSA_REP):
            s = _dot_nt(qa_ref[r], ka)
            if mode == "far":
                s = s + tab_ref[N_BUCKETS - 1, g * NSA_REP + r]
            else:
                s = s + _bias_tile(win_ref, r, qi - kt)
            if mode == "diag":
                s = jnp.where(col <= row, s, NEG)
            m_old = m_ref[r]
            m_new = jnp.maximum(m_old, jnp.max(s, axis=1, keepdims=True))
            alpha = jnp.exp(m_old - m_new)
            p = jnp.exp(s - m_new[:, 0:1])
            l_ref[r] = alpha * l_ref[r] + jnp.sum(p, axis=1, keepdims=True)
            acc_ref[r] = alpha[:, 0:HEAD_DIM] * acc_ref[r] + _dot(p.astype(BF16), v)
            m_ref[r] = m_new

    n_far = jnp.maximum(qi - (n_band - 1), 0)

    def far_body(kt, c):
        slc_tile(kt, "far")
        return c

    lax.fori_loop(0, n_far, far_body, 0)

    def band_body(kt, c):
        slc_tile(kt, "band")
        return c

    lax.fori_loop(n_far, qi, band_body, 0)
    slc_tile(qi, "diag")

    n_wt = WINDOW // TK + 1
    gate = _sigmoid(gate_ref[...])
    for r in range(NSA_REP):
        q = qa_ref[r, :, 0:HEAD_DIM]
        ss, vv = [], []
        for c in range(n_wt):
            delta = n_wt - 1 - c
            kt = qi - delta
            ktc = jnp.maximum(kt, 0)
            k0 = pl.multiple_of(ktc * TK, TK)
            s = _dot_nt(q, kw_ref[0, pl.ds(k0, TK), :]) + _bias_tile(win_ref, r, delta)
            dist = delta * TK + row - col
            ok = (dist >= 0) & (dist < WINDOW) & (kt >= 0)
            ss.append(jnp.where(ok, s, NEG))
            vv.append(vw_ref[0, pl.ds(k0, TK), :])
        mx = ss[0].max(axis=1, keepdims=True)
        for s in ss[1:]:
            mx = jnp.maximum(mx, s.max(axis=1, keepdims=True))
        den = jnp.zeros((TQ, 1), F32)
        o_w = jnp.zeros((TQ, HEAD_DIM), F32)
        for s, v in zip(ss, vv):
            e = jnp.where(s > 0.5 * NEG, jnp.exp(s - mx), 0.0)
            den = den + jnp.sum(e, axis=1, keepdims=True)
            o_w = o_w + _dot(e.astype(BF16), v)
        o_w = o_w / jnp.maximum(den, 1e-30)
        o_s = acc_ref[r] / jnp.maximum(l_ref[r][:, 0:1], 1e-30)
        o_c = oc_ref[:, r * HEAD_DIM:(r + 1) * HEAD_DIM]
        gates = [gate[:, 3 * r + c:3 * r + c + 1] for c in range(3)]
        o = gates[0] * o_c + gates[1] * o_s + gates[2] * o_w
        o_ref[:, r * HEAD_DIM:(r + 1) * HEAD_DIM] = o.astype(o_ref.dtype)


def _gate_weights(w_gate):
    k = w_gate.shape[0]
    w = w_gate.reshape(k, NSA_KV, 3 * NSA_REP)
    w = jnp.pad(w, ((0, 0), (0, 0), (0, LANE - 3 * NSA_REP)))
    return w.reshape(k, NSA_KV * LANE).astype(BF16)


def _nsa_augment(proj, ksl_head0, mask_t, onehot):
    t_len = proj.shape[1]
    n_sel = t_len // SEL_BLOCK
    pad = LANE - n_sel
    mask_neg = jnp.pad(mask_t.transpose(0, 2, 1), ((0, 0), (0, 0), (0, pad)))
    k_aug = jnp.concatenate(
        [proj[ksl_head0:ksl_head0 + NSA_KV],
         jnp.zeros((NSA_KV, t_len, LANE - HEAD_DIM), BF16),
         jnp.broadcast_to(jnp.pad(onehot, ((0, 0), (0, pad)))[None], (NSA_KV, t_len, LANE))], axis=2)
    return mask_neg, k_aug


def _slc_swa(proj, q_head0, mask_neg, k_aug, vs_head0, kw_head0, vw_head0, bias_win, rel_bias,
             gate_logits, o_cmp):
    t_len = proj.shape[1]
    n_sel = LANE
    hb = q_head0 // NSA_REP
    return pl.pallas_call(
        _slc_swa_kernel,
        grid=(NSA_KV, t_len // TQ),
        in_specs=[pl.BlockSpec((NSA_REP, TQ, HEAD_DIM), lambda g, i: (hb + g, i, 0)),
                  pl.BlockSpec((1, TQ, n_sel), lambda g, i: (g, i, 0)),
                  pl.BlockSpec((1, t_len, 2 * LANE), lambda g, i: (g, 0, 0)),
                  pl.BlockSpec((1, t_len, HEAD_DIM), lambda g, i: (vs_head0 + g, 0, 0)),
                  pl.BlockSpec((1, t_len, HEAD_DIM), lambda g, i: (kw_head0 + g, 0, 0)),
                  pl.BlockSpec((1, t_len, HEAD_DIM), lambda g, i: (vw_head0 + g, 0, 0)),
                  pl.BlockSpec((NSA_REP, N_BIAS_SUB, BIAS_SUB, BIAS_SUB), lambda g, i: (g, 0, 0, 0)),
                  pl.BlockSpec(memory_space=pltpu.SMEM),
                  pl.BlockSpec((TQ, LANE), lambda g, i: (i, g)),
                  pl.BlockSpec((TQ, NSA_REP * HEAD_DIM), lambda g, i: (i, g))],
        out_specs=pl.BlockSpec((TQ, NSA_REP * HEAD_DIM), lambda g, i: (i, g)),
        out_shape=jax.ShapeDtypeStruct((t_len, NSA_HEADS * HEAD_DIM), BF16),
        scratch_shapes=[pltpu.VMEM((NSA_REP, TQ, 2 * LANE), BF16),
                        pltpu.VMEM((NSA_REP, TQ, LANE), F32),
                        pltpu.VMEM((NSA_REP, TQ, LANE), F32),
                        pltpu.VMEM((NSA_REP, TQ, HEAD_DIM), F32)],
        compiler_params=_cparams(("parallel", "arbitrary")),
        name="nsa_slc_swa",
    )(proj, mask_neg, k_aug, proj, proj, proj, bias_win, rel_bias, gate_logits, o_cmp)


def _xattn_kernel(hb_ref, hf_ref, wq_ref, kv_ref, wo_ref, g_ref, b_ref, of_ref, ob_ref):
    q = _dot(hb_ref[...], wq_ref[...]).astype(BF16)
    scale = XA_HEAD_DIM ** -0.5
    outs = []
    for hh in range(XA_HEADS):
        lo = hh * XA_HEAD_DIM
        k = kv_ref[:, lo:lo + XA_HEAD_DIM]
        v = kv_ref[:, XA_W + lo:XA_W + lo + XA_HEAD_DIM]
        s = _dot_nt(q[:, lo:lo + XA_HEAD_DIM], k) * scale
        e = jnp.exp(s - jnp.max(s, axis=1, keepdims=True))
        p = e / jnp.sum(e, axis=1, keepdims=True)
        outs.append(_dot(p.astype(BF16), v).astype(BF16))
    o = jnp.concatenate(outs, axis=1)
    y = _layer_norm(ALPHA * hf_ref[...] + _dot(o, wo_ref[...]), g_ref[...], b_ref[...])
    of_ref[...] = y
    ob_ref[...] = y.astype(BF16)


def _xattn(hb, hf, wq, kv, wo, g, b, tm):
    m, n = hf.shape
    tm = min(tm, m)
    return pl.pallas_call(
        _xattn_kernel,
        grid=(m // tm,),
        in_specs=[pl.BlockSpec((tm, n), lambda i: (i, 0)),
                  pl.BlockSpec((tm, n), lambda i: (i, 0)),
                  pl.BlockSpec((n, XA_W), lambda i: (0, 0)),
                  pl.BlockSpec((N_MEM, 2 * XA_W), lambda i: (0, 0)),
                  pl.BlockSpec((XA_W, n), lambda i: (0, 0)),
                  pl.BlockSpec((1, n), lambda i: (0, 0)),
                  pl.BlockSpec((1, n), lambda i: (0, 0))],
        out_specs=[pl.BlockSpec((tm, n), lambda i: (i, 0)),
                   pl.BlockSpec((tm, n), lambda i: (i, 0))],
        out_shape=[jax.ShapeDtypeStruct((m, n), F32), jax.ShapeDtypeStruct((m, n), BF16)],
        compiler_params=_cparams(("parallel",)),
        name="xattn",
    )(hb, hf, wq, kv, wo, g.reshape(1, n), b.reshape(1, n))


def _ffn_up_kernel(h_ref, wa_ref, wg_ref, cwa_ref, cwg_ref, cba_ref, cbg_ref, o_ref, ta_ref, tg_ref):
    i = pl.program_id(1)

    @pl.when(i == 0)
    def _():
        ta_ref[...] = jnp.zeros_like(ta_ref)
        tg_ref[...] = jnp.zeros_like(tg_ref)

    hb = h_ref[...]
    tm = hb.shape[0]
    row = lax.broadcasted_iota(jnp.int32, (tm, o_ref.shape[1]), 0)

    def conv(w_ref, cw_ref, cb_ref, tail_ref):
        u = _dot(hb, w_ref[...])
        tail = tail_ref[...]
        u1 = jnp.where(row == 0, tail[SUBLANE - 1:SUBLANE, :], pltpu.roll(u, 1, 0))
        u2 = pltpu.roll(u, 2, 0)
        u2 = jnp.where(row == 0, tail[SUBLANE - 2:SUBLANE - 1, :], u2)
        u2 = jnp.where(row == 1, tail[SUBLANE - 1:SUBLANE, :], u2)
        tail_ref[...] = u[tm - SUBLANE:, :]
        return cw_ref[0:1, :] * u2 + cw_ref[1:2, :] * u1 + cw_ref[2:3, :] * u + cb_ref[...]

    a = conv(wa_ref, cwa_ref, cba_ref, ta_ref)
    gg = conv(wg_ref, cwg_ref, cbg_ref, tg_ref)
    o_ref[...] = (a * _gelu(gg)).astype(o_ref.dtype)


def _ffn_up(hb, w_up, conv_w, conv_b, tm, tn):
    m, k = hb.shape
    tm = min(tm, m)
    nj = D_FF // tn
    cb = conv_b.reshape(1, 2 * D_FF)
    return pl.pallas_call(
        _ffn_up_kernel,
        grid=(nj, m // tm),
        in_specs=[pl.BlockSpec((tm, k), lambda j, i: (i, 0)),
                  pl.BlockSpec((k, tn), lambda j, i: (0, j)),
                  pl.BlockSpec((k, tn), lambda j, i: (0, nj + j)),
                  pl.BlockSpec((3, tn), lambda j, i: (0, j)),
                  pl.BlockSpec((3, tn), lambda j, i: (0, nj + j)),
                  pl.BlockSpec((1, tn), lambda j, i: (0, j)),
                  pl.BlockSpec((1, tn), lambda j, i: (0, nj + j))],
        out_specs=pl.BlockSpec((tm, tn), lambda j, i: (i, j)),
        out_shape=jax.ShapeDtypeStruct((m, D_FF), BF16),
        scratch_shapes=[pltpu.VMEM((SUBLANE, tn), F32), pltpu.VMEM((SUBLANE, tn), F32)],
        compiler_params=_cparams(("parallel", "arbitrary")),
        name="ffn_up",
    )(hb, w_up, w_up, conv_w, conv_w, cb, cb)


def _static_tables(t_len):
    n16 = t_len // CMP_STRIDE
    n_sel = t_len // SEL_BLOCK
    n_cmp = (t_len - CMP_LEN) // CMP_STRIDE + 1
    cmp_start = np.arange(n16) * CMP_STRIDE
    cmp_end = cmp_start + CMP_LEN - 1
    sel = np.arange(n_sel)
    overlap_t = ((cmp_start[None, :] < (sel[:, None] + 1) * SEL_BLOCK)
                 & (cmp_end[None, :] >= sel[:, None] * SEL_BLOCK)
                 & (np.arange(n16)[None, :] < n_cmp))
    tri = np.arange(TK)[:, None] > np.arange(TK)[None, :]
    onehot = (np.arange(t_len)[:, None] // SEL_BLOCK) == sel[None, :]
    return (jnp.asarray(overlap_t, BF16), jnp.asarray(tri, BF16), jnp.asarray(onehot, BF16))


def kernel(x, mem, w_in, w_out, s5_lambda_re, s5_lambda_im, s5_log_dt, s5_b_re, s5_b_im, s5_c_re, s5_c_im, s5_d, s5_w_glu, s5_b_glu, nsa_cmp_pos, nsa_cmp_w1, nsa_cmp_w2, rel_bias, xa_wq, xa_wkv, xa_wo, ffn_w_up, ffn_conv_w, ffn_conv_b, ffn_w_down, ln_g, ln_b):
    t_len = x.shape[1]
    n_sel = t_len // SEL_BLOCK
    overlap_t, tri, onehot = _static_tables(t_len)
    bias_win, bias_cmp = _bias_tables(rel_bias)
    mem_b = mem[0].astype(BF16)
    hf = x[0]
    hb = hf.astype(BF16)
    n_head_cols = (3 * SB_HEADS + NSA_HEADS + 6 * NSA_KV) * HEAD_DIM
    c_heads = S5_CH
    c_gate = S5_CH + n_head_cols
    h_nq = 3 * SB_HEADS
    h_cmp = h_nq + NSA_HEADS
    h_ksl = h_cmp + 2 * NSA_KV
    h_vsl = h_ksl + NSA_KV
    h_ksw = h_vsl + NSA_KV
    h_vsw = h_ksw + NSA_KV
    for l in range(DEPTH):
        w_l = w_in[l]
        u5 = _mm(hb, w_l[:, :c_heads].astype(BF16), F32, 512, 512)
        proj = _mm_heads(hb, w_l[:, c_heads:c_gate].astype(BF16), 512)
        gate_logits = _mm(hb, _gate_weights(w_l[:, c_gate:]), F32, 512, LANE)

        s5p = _s5_params(s5_lambda_re[l], s5_lambda_im[l], s5_log_dt[l], s5_b_re[l], s5_b_im[l],
                         s5_c_re[l], s5_c_im[l])
        y_s5 = _s5(u5, s5p, s5_w_glu[l], s5_b_glu[l], s5_d[l])
        y_sb = _stick_breaking(proj, tri).transpose(1, 0, 2).reshape(t_len, SB_HEADS * HEAD_DIM)

        kvc = _compress(proj, h_cmp, nsa_cmp_pos[l], nsa_cmp_w1[l], nsa_cmp_w2[l])
        o_cmp, mask_t = _cmp_select(proj, h_nq, kvc, bias_cmp, rel_bias, overlap_t)
        mask_neg, k_aug = _nsa_augment(proj, h_ksl, mask_t, onehot)
        y_nsa = _slc_swa(proj, h_nq, mask_neg, k_aug, h_vsl, h_ksw, h_vsw, bias_win, rel_bias,
                         gate_logits, o_cmp)

        y = jnp.concatenate([y_s5, y_sb, y_nsa], axis=1)
        hf, hb = _mm_ln(y, w_out[l].astype(BF16), hf, ln_g[l, 0], ln_b[l, 0], 512, 512)

        kv = _mm(mem_b, xa_wkv[l].astype(BF16), BF16, N_MEM, 512)
        hf, hb = _xattn(hb, hf, xa_wq[l].astype(BF16), kv, xa_wo[l].astype(BF16),
                        ln_g[l, 1], ln_b[l, 1], 256)

        act = _ffn_up(hb, ffn_w_up[l].astype(BF16), ffn_conv_w[l], ffn_conv_b[l], 512, 512)
        hf, hb = _mm_ln(act, ffn_w_down[l].astype(BF16), hf, ln_g[l, 2], ln_b[l, 2], 512, 512)
    return hf[None]
```

```python
import functools
import math

import numpy as np
import jax
import jax.numpy as jnp
from jax import lax
from jax.experimental import pallas as pl
from jax.experimental.pallas import tpu as pltpu

F32 = jnp.float32
BF16 = jnp.bfloat16

D_MODEL = 2048
DEPTH = 4
HEAD_DIM = 64
S5_CH = 512
S5_GROUP = 16
S5_NG = 32
S5_P = 64
SB_HEADS = 8
NSA_HEADS = 16
NSA_KV = 4
NSA_REP = 4
CMP_LEN = 32
CMP_STRIDE = 16
CMP_HID = 128
SEL_BLOCK = 64
SEL_TOPN = 16
WINDOW = 512
FORCE_SCORE = 1e4
N_BUCKETS = 32
MAX_DIST = 1024
N_MEM = 256
XA_HEADS = 4
XA_HEAD_DIM = 128
XA_W = 512
D_FF = 5632
ALPHA = (2.0 * DEPTH) ** 0.25
LN_EPS = 1e-5
NEG = -1e30

LANE = 128
SUBLANE = 8
MXU_DIM = 256
VMEM_LIMIT = 56 * 1024 * 1024
TQ = 256
TK = 256
HEADS_PER_STEP = 4
S5_SEG = 64
S5_CHUNK = SUBLANE * S5_SEG
S5_CB = 8
BIAS_SUB = 128
N_BIAS_SUB = 10

HT_SBQ = 0
HT_NQ = HT_SBQ + SB_HEADS
HT_SBV = HT_NQ + NSA_HEADS
HT_VSL = HT_SBV + SB_HEADS
HT_VSW = HT_VSL + NSA_KV
HT_ROWS = (HT_VSW + NSA_KV) * HEAD_DIM
HN_SBK = 0
HN_KC = HN_SBK + SB_HEADS
HN_KSL = HN_KC + 2 * NSA_KV
HN_KSW = HN_KSL + NSA_KV
HN_HEADS = HN_KSW + NSA_KV


def _cparams(sem):
    return pltpu.CompilerParams(dimension_semantics=sem, vmem_limit_bytes=VMEM_LIMIT)


def _dot(a, b):
    return jnp.dot(a, b, preferred_element_type=F32)


def _dot_nt(a, b):
    return lax.dot_general(a, b, (((1,), (1,)), ((), ())), preferred_element_type=F32)


def _split_bf16(x):
    hi = x.astype(BF16)
    lo = (x - hi.astype(F32)).astype(BF16)
    return hi, lo


def _gelu(x):
    c = math.sqrt(2.0 / math.pi)
    return 0.5 * x * (1.0 + jnp.tanh(c * (x + 0.044715 * (x * x * x))))


def _sigmoid(x):
    return 1.0 / (1.0 + jnp.exp(-x))


def _layer_norm(x, g, b):
    mu = jnp.mean(x, axis=-1, keepdims=True)
    xc = x - mu
    var = jnp.mean(xc * xc, axis=-1, keepdims=True)
    return xc * lax.rsqrt(var + LN_EPS) * g + b


def _mm_kernel(a_ref, b_ref, o_ref):
    o_ref[...] = _dot(a_ref[...], b_ref[...]).astype(o_ref.dtype)


def _mm(a, b, out_dtype, tm, tn):
    m, k = a.shape
    n = b.shape[1]
    tm = min(tm, m)
    return pl.pallas_call(
        _mm_kernel,
        grid=(n // tn, m // tm),
        in_specs=[pl.BlockSpec((tm, k), lambda j, i: (i, 0)),
                  pl.BlockSpec((k, tn), lambda j, i: (0, j))],
        out_specs=pl.BlockSpec((tm, tn), lambda j, i: (i, j)),
        out_shape=jax.ShapeDtypeStruct((m, n), out_dtype),
        compiler_params=_cparams(("parallel", "arbitrary")),
        name="mm",
    )(a, b)


def _mm_heads_kernel(a_ref, b_ref, o_ref):
    o = _dot(a_ref[...], b_ref[...]).astype(o_ref.dtype)
    for r in range(o_ref.shape[0]):
        o_ref[r] = o[:, r * HEAD_DIM:(r + 1) * HEAD_DIM]


def _mm_heads(a, b, tm):
    m, k = a.shape
    n = b.shape[1]
    tn = MXU_DIM
    hpt = tn // HEAD_DIM
    tm = min(tm, m)
    return pl.pallas_call(
        _mm_heads_kernel,
        grid=(n // tn, m // tm),
        in_specs=[pl.BlockSpec((tm, k), lambda j, i: (i, 0)),
                  pl.BlockSpec((k, tn), lambda j, i: (0, j))],
        out_specs=pl.BlockSpec((hpt, tm, HEAD_DIM), lambda j, i: (j, i, 0)),
        out_shape=jax.ShapeDtypeStruct((n // HEAD_DIM, m, HEAD_DIM), BF16),
        compiler_params=_cparams(("parallel", "arbitrary")),
        name="mm_heads",
    )(a, b)


def _mm_nt_kernel(w_ref, a_ref, o_ref):
    o_ref[0] = _dot_nt(w_ref[...], a_ref[...]).astype(o_ref.dtype)


def _mm_nt(w_t, a, out_dtype, tn):
    n, k = w_t.shape
    m = a.shape[0]
    return pl.pallas_call(
        _mm_nt_kernel,
        grid=(n // tn, m // TQ),
        in_specs=[pl.BlockSpec((tn, k), lambda j, i: (j, 0)),
                  pl.BlockSpec((TQ, k), lambda j, i: (i, 0))],
        out_specs=pl.BlockSpec((1, tn, TQ), lambda j, i: (i, j, 0)),
        out_shape=jax.ShapeDtypeStruct((m // TQ, n, TQ), out_dtype),
        compiler_params=_cparams(("parallel", "arbitrary")),
        name="mm_nt",
    )(w_t, a)


def _mm_ln_kernel(a_ref, w_ref, h_ref, g_ref, b_ref, of_ref, ob_ref, acc_ref):
    kk = pl.program_id(1)

    @pl.when(kk == 0)
    def _():
        acc_ref[...] = jnp.zeros_like(acc_ref)

    acc_ref[...] += _dot(a_ref[...], w_ref[...])

    @pl.when(kk == pl.num_programs(1) - 1)
    def _():
        y = _layer_norm(ALPHA * h_ref[...] + acc_ref[...], g_ref[...], b_ref[...])
        of_ref[...] = y
        ob_ref[...] = y.astype(BF16)


def _mm_ln(a, w, h, g, b, tm, tk):
    m, k = a.shape
    n = w.shape[1]
    tm = min(tm, m)
    return pl.pallas_call(
        _mm_ln_kernel,
        grid=(m // tm, k // tk),
        in_specs=[pl.BlockSpec((tm, tk), lambda i, kk: (i, kk)),
                  pl.BlockSpec((tk, n), lambda i, kk: (kk, 0)),
                  pl.BlockSpec((tm, n), lambda i, kk: (i, 0)),
                  pl.BlockSpec((1, n), lambda i, kk: (0, 0)),
                  pl.BlockSpec((1, n), lambda i, kk: (0, 0))],
        out_specs=[pl.BlockSpec((tm, n), lambda i, kk: (i, 0)),
                   pl.BlockSpec((tm, n), lambda i, kk: (i, 0))],
        out_shape=[jax.ShapeDtypeStruct((m, n), F32), jax.ShapeDtypeStruct((m, n), BF16)],
        scratch_shapes=[pltpu.VMEM((tm, n), F32)],
        compiler_params=_cparams(("parallel", "arbitrary")),
        name="mm_ln",
    )(a, w, h, g.reshape(1, n), b.reshape(1, n))


def _s5_kernel(u_ref, bm_ref, cm_ref, a_ref, d_ref, z_ref, s_ref, p_ref, carry_ref):
    tc = pl.program_id(1)
    w = s_ref.shape[1] // 2
    n_step = p_ref.shape[0]
    ar1 = a_ref[0:1, :]
    ai1 = a_ref[1:2, :]

    @pl.when(tc == 0)
    def _():
        carry_ref[...] = jnp.zeros_like(carry_ref)
        p_ref[0:1, 0:w] = ar1
        p_ref[0:1, w:2 * w] = ai1

        def pw(t, c):
            pr, pi = c
            nr = pr * ar1 - pi * ai1
            ni = pr * ai1 + pi * ar1
            p_ref[pl.ds(t, 1), 0:w] = nr
            p_ref[pl.ds(t, 1), w:2 * w] = ni
            return nr, ni

        lax.fori_loop(1, n_step, pw, (ar1, ai1))

    u = u_ref[...]
    s_ref[...] = _dot(u.astype(BF16), bm_ref[0])

    ar = jnp.broadcast_to(ar1, (SUBLANE, w))
    ai = jnp.broadcast_to(ai1, (SUBLANE, w))

    def step(t, c):
        sr, si = c
        r0 = pl.multiple_of(t * SUBLANE, SUBLANE)
        xr = s_ref[pl.ds(r0, SUBLANE), 0:w]
        xi = s_ref[pl.ds(r0, SUBLANE), w:2 * w]
        nr = ar * sr - ai * si + xr
        ni = ar * si + ai * sr + xi
        s_ref[pl.ds(r0, SUBLANE), 0:w] = nr
        s_ref[pl.ds(r0, SUBLANE), w:2 * w] = ni
        return nr, ni

    zero = jnp.zeros((SUBLANE, w), F32)
    fr, fi = lax.fori_loop(0, n_step, step, (zero, zero), unroll=8)

    amr = p_ref[n_step - 1:n_step, 0:w]
    ami = p_ref[n_step - 1:n_step, w:2 * w]
    row = lax.broadcasted_iota(jnp.int32, (SUBLANE, w), 0)
    cr = carry_ref[0:1, 0:w]
    ci = carry_ref[0:1, w:2 * w]
    cin_r = jnp.zeros((SUBLANE, w), F32)
    cin_i = jnp.zeros((SUBLANE, w), F32)
    for seg in range(SUBLANE):
        cin_r = jnp.where(row == seg, cr, cin_r)
        cin_i = jnp.where(row == seg, ci, cin_i)
        nr = fr[seg:seg + 1, :] + amr * cr - ami * ci
        ni = fi[seg:seg + 1, :] + amr * ci + ami * cr
        cr, ci = nr, ni
    carry_ref[0:1, 0:w] = cr
    carry_ref[0:1, w:2 * w] = ci

    def fix(t, c):
        r0 = pl.multiple_of(t * SUBLANE, SUBLANE)
        pr = p_ref[pl.ds(t, 1), 0:w]
        pi = p_ref[pl.ds(t, 1), w:2 * w]
        s_ref[pl.ds(r0, SUBLANE), 0:w] += pr * cin_r - pi * cin_i
        s_ref[pl.ds(r0, SUBLANE), w:2 * w] += pr * cin_i + pi * cin_r
        return c

    lax.fori_loop(0, n_step, fix, 0, unroll=8)

    y = _dot(s_ref[...].astype(BF16), cm_ref[0]) + d_ref[...] * u
    z_ref[...] = _gelu(y)


def _s5_glu_kernel(z_ref, w_ref, b_ref, o_ref):
    z = z_ref[...]
    gate = _sigmoid(_dot(z.astype(BF16), w_ref[...]) + b_ref[...])
    o_ref[...] = (z * gate).astype(o_ref.dtype)


def _s5_params(lam_re, lam_im, log_dt, b_re, b_im, c_re, c_im):
    delta = jnp.exp(log_dt)[:, None]
    mag = jnp.exp(lam_re * delta)
    ar = mag * jnp.cos(lam_im * delta)
    ai = mag * jnp.sin(lam_im * delta)
    den = lam_re * lam_re + lam_im * lam_im
    cr = ((ar - 1.0) * lam_re + ai * lam_im) / den
    ci = (ai * lam_re - (ar - 1.0) * lam_im) / den
    br = cr[..., None] * b_re - ci[..., None] * b_im
    bi = cr[..., None] * b_im + ci[..., None] * b_re
    ncb = S5_NG // S5_CB
    eye = jnp.eye(S5_CB, dtype=F32)

    def in_mat(b):
        b = b.reshape(ncb, S5_CB, S5_P, S5_GROUP)
        m = jnp.einsum('ngpc,gh->ngchp', b, eye)
        return m.reshape(ncb, S5_CB * S5_GROUP, S5_CB * S5_P)

    def out_mat(c):
        c = c.reshape(ncb, S5_CB, S5_GROUP, S5_P)
        m = jnp.einsum('ngcp,gh->ngphc', c, eye)
        return m.reshape(ncb, S5_CB * S5_P, S5_CB * S5_GROUP)

    bm = jnp.concatenate([in_mat(br), in_mat(bi)], axis=2).astype(BF16)
    cm = jnp.concatenate([out_mat(c_re), -out_mat(c_im)], axis=1).astype(BF16)
    a = jnp.stack([ar.reshape(-1), ai.reshape(-1)], axis=0)
    return bm, cm, a


def _s5(u, p, w_glu, b_glu, d):
    t_len = u.shape[0]
    nc = t_len // S5_CHUNK
    ncb = S5_NG // S5_CB
    wl = S5_CB * S5_GROUP
    ws = S5_CB * S5_P
    bm, cm, a = p
    up = u.reshape(nc, SUBLANE, S5_SEG, S5_CH).transpose(0, 2, 1, 3).reshape(t_len, S5_CH)
    z = pl.pallas_call(
        _s5_kernel,
        grid=(ncb, nc),
        in_specs=[pl.BlockSpec((S5_CHUNK, wl), lambda c, t: (t, c)),
                  pl.BlockSpec((1, wl, 2 * ws), lambda c, t: (c, 0, 0)),
                  pl.BlockSpec((1, 2 * ws, wl), lambda c, t: (c, 0, 0)),
                  pl.BlockSpec((2, ws), lambda c, t: (0, c)),
                  pl.BlockSpec((1, wl), lambda c, t: (0, c))],
        out_specs=pl.BlockSpec((S5_CHUNK, wl), lambda c, t: (t, c)),
        out_shape=jax.ShapeDtypeStruct((t_len, S5_CH), F32),
        scratch_shapes=[pltpu.VMEM((S5_CHUNK, 2 * ws), F32),
                        pltpu.VMEM((S5_SEG, 2 * ws), F32),
                        pltpu.VMEM((SUBLANE, 2 * ws), F32)],
        compiler_params=_cparams(("parallel", "arbitrary")),
        name="s5_scan",
    )(up, bm, cm, a, d.reshape(1, S5_CH))
    tm = min(1024, t_len)
    y = pl.pallas_call(
        _s5_glu_kernel,
        grid=(t_len // tm,),
        in_specs=[pl.BlockSpec((tm, S5_CH), lambda i: (i, 0)),
                  pl.BlockSpec((S5_CH, S5_CH), lambda i: (0, 0)),
                  pl.BlockSpec((1, S5_CH), lambda i: (0, 0))],
        out_specs=pl.BlockSpec((tm, S5_CH), lambda i: (i, 0)),
        out_shape=jax.ShapeDtypeStruct((t_len, S5_CH), BF16),
        compiler_params=_cparams(("parallel",)),
        name="s5_glu",
    )(z, w_glu.astype(BF16), b_glu.reshape(1, S5_CH))
    return y.reshape(nc, S5_SEG, SUBLANE, S5_CH).transpose(0, 2, 1, 3).reshape(t_len, S5_CH)


def _sb_kernel(q_ref, k_ref, v_ref, u_ref, o_ref, r_ref, acc_ref, nl_ref, arg_ref, w_ref):
    qi = pl.program_id(1)
    ucat = u_ref[...]
    qs = [q_ref[0, r * HEAD_DIM:(r + 1) * HEAD_DIM, :] * 0.125 for r in range(HEADS_PER_STEP)]
    r_ref[...] = jnp.zeros_like(r_ref)
    acc_ref[...] = jnp.zeros_like(acc_ref)
    key = lax.broadcasted_iota(jnp.int32, (TK, TQ), 0)
    qry = lax.broadcasted_iota(jnp.int32, (TK, TQ), 1)
    mask = key < qry

    def tile(kt, diag):
        k0 = pl.multiple_of(kt * TK, TK)
        sums = []
        for r in range(HEADS_PER_STEP):
            z = _dot(k_ref[r, pl.ds(k0, TK), :], qs[r])
            nl = jnp.maximum(z, 0.0) + jnp.log(1.0 + jnp.exp(-jnp.abs(z)))
            arg_ref[r] = z - nl
            if diag:
                nl = jnp.where(mask, nl, 0.0)
            hi, lo = _split_bf16(nl)
            nl_ref[r, 0:TK, :] = hi
            nl_ref[r, TK:2 * TK, :] = lo
            sums.append(jnp.sum(nl, axis=0, keepdims=True))
        for r in range(HEADS_PER_STEP):
            later = _dot(ucat, nl_ref[r])
            w = jnp.exp(arg_ref[r] - later)
            if diag:
                w = jnp.where(mask, w, 0.0)
            w_ref[r] = w.astype(BF16)
        for r in range(HEADS_PER_STEP):
            v = v_ref[kt, r * HEAD_DIM:(r + 1) * HEAD_DIM, :]
            r_old = r_ref[r, 0:1, :]
            acc_ref[r] += _dot(v, w_ref[r]) * jnp.exp(-r_old)
            r_ref[r, 0:1, :] = r_old + sums[r]

    tile(qi, True)

    def body(n, c):
        tile(qi - 1 - n, False)
        return c

    lax.fori_loop(0, qi, body, 0)
    o_t = jnp.concatenate([acc_ref[r] for r in range(HEADS_PER_STEP)], axis=0)
    o_ref[...] = o_t.T.astype(o_ref.dtype)


def _stick_breaking(proj, proj_t, ucat):
    t_len = proj.shape[1]
    nt = t_len // TQ
    hw = HEADS_PER_STEP * HEAD_DIM
    return pl.pallas_call(
        _sb_kernel,
        grid=(SB_HEADS // HEADS_PER_STEP, nt),
        in_specs=[pl.BlockSpec((1, hw, TQ), lambda h, i: (i, HT_SBQ // HEADS_PER_STEP + h, 0)),
                  pl.BlockSpec((HEADS_PER_STEP, t_len, HEAD_DIM),
                               lambda h, i: (HN_SBK // HEADS_PER_STEP + h, 0, 0)),
                  pl.BlockSpec((nt, hw, TQ), lambda h, i: (0, HT_SBV // HEADS_PER_STEP + h, 0)),
                  pl.BlockSpec((TK, 2 * TK), lambda h, i: (0, 0))],
        out_specs=pl.BlockSpec((TQ, hw), lambda h, i: (i, h)),
        out_shape=jax.ShapeDtypeStruct((t_len, SB_HEADS * HEAD_DIM), BF16),
        scratch_shapes=[pltpu.VMEM((HEADS_PER_STEP, SUBLANE, TQ), F32),
                        pltpu.VMEM((HEADS_PER_STEP, HEAD_DIM, TQ), F32),
                        pltpu.VMEM((HEADS_PER_STEP, 2 * TK, TQ), BF16),
                        pltpu.VMEM((HEADS_PER_STEP, TK, TQ), F32),
                        pltpu.VMEM((HEADS_PER_STEP, TK, TQ), BF16)],
        compiler_params=_cparams(("parallel", "arbitrary")),
        name="stick_breaking",
    )(proj_t, proj, proj_t, ucat)


def _t5_bucket(dist):
    n = jnp.maximum(dist, 0)
    max_exact = N_BUCKETS // 2
    nf = jnp.maximum(n, 1).astype(jnp.float32)
    large = max_exact + (jnp.log(nf / max_exact) / math.log(MAX_DIST / max_exact)
                         * (N_BUCKETS - max_exact)).astype(jnp.int32)
    large = jnp.minimum(large, N_BUCKETS - 1)
    return jnp.where(n < max_exact, n, large)


def _bias_lookup_kernel(tab_ref, idx_ref, o_ref):
    h = pl.program_id(0)
    idx = idx_ref[...]
    acc = jnp.zeros(idx.shape, F32)
    for b in range(N_BUCKETS):
        acc = jnp.where(idx == b, tab_ref[b, h], acc)
    o_ref[0] = acc


def _bias_lookup(tab, idx):
    r, c = idx.shape
    n_h = tab.shape[1]
    return pl.pallas_call(
        _bias_lookup_kernel,
        grid=(n_h,),
        in_specs=[pl.BlockSpec(memory_space=pltpu.SMEM),
                  pl.BlockSpec((r, c), lambda h: (0, 0))],
        out_specs=pl.BlockSpec((1, r, c), lambda h: (h, 0, 0)),
        out_shape=jax.ShapeDtypeStruct((n_h, r, c), F32),
        compiler_params=_cparams(("arbitrary",)),
        name="bias_lookup",
    )(tab, idx)


def _bias_tables(rel_bias):
    i = jnp.arange(BIAS_SUB)
    d_win = (BIAS_SUB * jnp.arange(N_BIAS_SUB)[:, None, None] + i[None, None, :] - i[None, :, None])
    idx_win = _t5_bucket(d_win).reshape(N_BIAS_SUB * BIAS_SUB, BIAS_SUB)
    win = _bias_lookup(rel_bias, idx_win).reshape(NSA_HEADS, N_BIAS_SUB, BIAS_SUB, BIAS_SUB)
    n_m = (LANE * CMP_STRIDE) // TQ
    m = jnp.arange(n_m)[:, None, None]
    c = jnp.arange(2 * LANE)[None, :, None]
    qi = jnp.arange(TQ)[None, None, :]
    d_cmp = TQ * m + qi + LANE * CMP_STRIDE - CMP_STRIDE * c - (CMP_LEN - 1)
    idx_cmp = _t5_bucket(d_cmp).reshape(n_m * 2 * LANE, TQ)
    cmp_t = _bias_lookup(rel_bias, idx_cmp).reshape(NSA_HEADS, n_m, 2 * LANE, TQ)
    return win, cmp_t


def _compress_kernel(c0_ref, c1_ref, w1_ref, pos_ref, w2_ref, w2t_ref, o_ref, ot_ref):
    half = c0_ref.shape[3]
    w1 = w1_ref[0]
    posb = _dot(pos_ref[0], w1)[0:1, :]
    hid = _dot(c0_ref[0, 0], w1[0:half, :]) + _dot(c1_ref[0, 0], w1[half:, :]) + posb
    act = _gelu(hid).astype(BF16)
    o_ref[0, 0] = _dot(act, w2_ref[0]).astype(o_ref.dtype)
    ot_ref[0, 0] = _dot_nt(w2t_ref[0], act).astype(ot_ref.dtype)


def _compress(proj, head0, cmp_pos, cmp_w1, cmp_w2):
    t_len = proj.shape[1]
    n16 = t_len // CMP_STRIDE
    half = CMP_STRIDE * HEAD_DIM
    kv = proj[head0:head0 + 2 * NSA_KV].reshape(2, NSA_KV, n16, half)
    kv_next = jnp.concatenate([kv[:, :, 1:], jnp.zeros((2, NSA_KV, 1, half), BF16)], axis=2)
    pos = jnp.broadcast_to(cmp_pos.reshape(2, 1, CMP_LEN * HEAD_DIM), (2, SUBLANE, CMP_LEN * HEAD_DIM))
    w2 = cmp_w2.astype(BF16)
    return pl.pallas_call(
        _compress_kernel,
        grid=(2, NSA_KV),
        in_specs=[pl.BlockSpec((1, 1, n16, half), lambda j, g: (j, g, 0, 0)),
                  pl.BlockSpec((1, 1, n16, half), lambda j, g: (j, g, 0, 0)),
                  pl.BlockSpec((1, 2 * half, CMP_HID), lambda j, g: (j, 0, 0)),
                  pl.BlockSpec((1, SUBLANE, 2 * half), lambda j, g: (j, 0, 0)),
                  pl.BlockSpec((1, CMP_HID, HEAD_DIM), lambda j, g: (j, 0, 0)),
                  pl.BlockSpec((1, HEAD_DIM, CMP_HID), lambda j, g: (j, 0, 0))],
        out_specs=[pl.BlockSpec((1, 1, n16, HEAD_DIM), lambda j, g: (j, g, 0, 0)),
                   pl.BlockSpec((1, 1, HEAD_DIM, n16), lambda j, g: (j, g, 0, 0))],
        out_shape=[jax.ShapeDtypeStruct((2, NSA_KV, n16, HEAD_DIM), BF16),
                   jax.ShapeDtypeStruct((2, NSA_KV, HEAD_DIM, n16), BF16)],
        compiler_params=_cparams(("parallel", "arbitrary")),
        name="nsa_compress",
    )(kv, kv_next, cmp_w1.astype(BF16), pos.astype(BF16), w2, w2.transpose(0, 2, 1))


def _cmp_select_kernel(q_ref, kc_ref, vct_ref, bias_ref, tab_ref, ov_ref, oc_ref, mn_ref, *, n_cmp):
    g = pl.program_id(0)
    qi = pl.program_id(1)
    n16 = kc_ref.shape[2]
    n_ct = n16 // LANE
    kc = kc_ref[0, 0]
    vct = vct_ref[0, 0]
    q_per_ct = (LANE * CMP_STRIDE) // TQ
    jd = qi // q_per_ct
    n = lax.broadcasted_iota(jnp.int32, (n16, TQ), 0)
    t = qi * TQ + lax.broadcasted_iota(jnp.int32, (n16, TQ), 1)
    valid = (CMP_STRIDE * n + (CMP_LEN - 1) <= t) & (n < n_cmp)
    psum = jnp.zeros((n16, TQ), F32)
    for r in range(NSA_REP):
        q = q_ref[0, r * HEAD_DIM:(r + 1) * HEAD_DIM, :] * 0.125
        far = tab_ref[N_BUCKETS - 1, g * NSA_REP + r]
        rows = []
        for jc in range(n_ct):
            near = jnp.where(jc == jd, bias_ref[r, 0, LANE:2 * LANE, :], bias_ref[r, 0, 0:LANE, :])
            rows.append(jnp.where((jc == jd) | (jc == jd - 1), near, far))
        bias = rows[0] if n_ct == 1 else jnp.concatenate(rows, axis=0)
        s = jnp.where(valid, _dot(kc, q) + bias, NEG)
        mx = jnp.max(s, axis=0, keepdims=True)
        e = jnp.where(valid, jnp.exp(s - mx), 0.0)
        p = e * (1.0 / jnp.maximum(jnp.sum(e, axis=0, keepdims=True), 1e-30))
        psum = psum + p
        oc_ref[r * HEAD_DIM:(r + 1) * HEAD_DIM, :] = _dot(vct, p.astype(BF16))
    hi, lo = _split_bf16(psum)
    imp = _dot(ov_ref[...], jnp.concatenate([hi, lo], axis=0))
    n_sel = imp.shape[0]
    j = lax.broadcasted_iota(jnp.int32, (n_sel, TQ), 0)
    tq = qi * TQ + lax.broadcasted_iota(jnp.int32, (n_sel, TQ), 1)
    cur = tq // SEL_BLOCK
    forced = (j == 0) | (j == cur) | (j == cur - 1)
    score = jnp.where(j <= cur, jnp.where(forced, FORCE_SCORE, imp), -1.0)
    jf = j.astype(F32)
    sel = jnp.zeros((n_sel, TQ), F32)
    for _ in range(min(SEL_TOPN, n_sel)):
        mx = jnp.max(score, axis=0, keepdims=True)
        first = jnp.min(jnp.where(score == mx, jf, float(n_sel)), axis=0, keepdims=True)
        hit = jf == first
        sel = jnp.where(hit, 1.0, sel)
        score = jnp.where(hit, -jnp.inf, score)
    mn_ref[0] = (sel - 1.0).astype(mn_ref.dtype)


def _cmp_select(proj_t, kc, vct, bias_cmp, rel_bias, overlap_cat):
    nt = proj_t.shape[0]
    t_len = nt * TQ
    n16 = kc.shape[2]
    n_sel = t_len // SEL_BLOCK
    n_cmp = (t_len - CMP_LEN) // CMP_STRIDE + 1
    q_per_ct = (LANE * CMP_STRIDE) // TQ
    hw = NSA_REP * HEAD_DIM
    return pl.pallas_call(
        functools.partial(_cmp_select_kernel, n_cmp=n_cmp),
        grid=(NSA_KV, nt),
        in_specs=[pl.BlockSpec((1, hw, TQ), lambda g, i: (i, HT_NQ // NSA_REP + g, 0)),
                  pl.BlockSpec((1, 1, n16, HEAD_DIM), lambda g, i: (0, g, 0, 0)),
                  pl.BlockSpec((1, 1, HEAD_DIM, n16), lambda g, i: (1, g, 0, 0)),
                  pl.BlockSpec((NSA_REP, 1, 2 * LANE, TQ), lambda g, i: (g, i % q_per_ct, 0, 0)),
                  pl.BlockSpec(memory_space=pltpu.SMEM),
                  pl.BlockSpec((n_sel, 2 * n16), lambda g, i: (0, 0))],
        out_specs=[pl.BlockSpec((hw, TQ), lambda g, i: (g, i)),
                   pl.BlockSpec((1, n_sel, TQ), lambda g, i: (g, 0, i))],
        out_shape=[jax.ShapeDtypeStruct((NSA_HEADS * HEAD_DIM, t_len), F32),
                   jax.ShapeDtypeStruct((NSA_KV, n_sel, t_len), BF16)],
        compiler_params=_cparams(("parallel", "arbitrary")),
        name="nsa_cmp_select",
    )(proj_t, kc, vct, bias_cmp, rel_bias, overlap_cat)


def _bias_tile(win_ref, r, delta):
    sub = TQ // BIAS_SUB
    rows = []
    for b in range(sub):
        cols = []
        for a in range(sub):
            idx = jnp.maximum(sub * delta + a - b, 0)
            cols.append(win_ref[r, idx])
        rows.append(jnp.concatenate(cols, axis=1))
    return jnp.concatenate(rows, axis=0)


def _slc_swa_kernel(q_ref, mn_ref, ka_ref, vs_ref, kw_ref, vw_ref, win_ref, tab_ref, gate_ref,
                    oc_ref, o_ref, qa_ref, m_ref, l_ref, acc_ref, s_ref, p_ref):
    g = pl.program_id(0)
    qi = pl.program_id(1)
    n_sel = mn_ref.shape[1]
    n_band = (N_BIAS_SUB * BIAS_SUB) // TK
    big = mn_ref[0] * jnp.asarray(1e30, BF16)
    for r in range(NSA_REP):
        qa_ref[r, 0:HEAD_DIM, :] = q_ref[0, r * HEAD_DIM:(r + 1) * HEAD_DIM, :] * 0.125
        qa_ref[r, HEAD_DIM:LANE, :] = jnp.zeros((LANE - HEAD_DIM, TQ), BF16)
        qa_ref[r, LANE:LANE + n_sel, :] = big
        if n_sel < LANE:
            qa_ref[r, LANE + n_sel:2 * LANE, :] = jnp.zeros((LANE - n_sel, TQ), BF16)
    m_ref[...] = jnp.full_like(m_ref, NEG)
    l_ref[...] = jnp.zeros_like(l_ref)
    acc_ref[...] = jnp.zeros_like(acc_ref)
    key = lax.broadcasted_iota(jnp.int32, (TK, TQ), 0)
    qry = lax.broadcasted_iota(jnp.int32, (TK, TQ), 1)

    def slc_tile(kt, mode):
        k0 = pl.multiple_of(kt * TK, TK)
        ka = ka_ref[0, pl.ds(k0, TK), :]
        v = vs_ref[kt]
        stats = []
        for r in range(NSA_REP):
            s = _dot(ka, qa_ref[r])
            if mode != "far":
                s = s + _bias_tile(win_ref, r, qi - kt)
            if mode == "diag":
                s = jnp.where(key <= qry, s, NEG)
            s_ref[r] = s
            m_old = m_ref[r, 0:1, :]
            m_new = jnp.maximum(m_old, jnp.max(s, axis=0, keepdims=True))
            m_ref[r, 0:1, :] = m_new
            stats.append((m_new, jnp.exp(m_old - m_new)))
        for r in range(NSA_REP):
            m_new, alpha = stats[r]
            p = jnp.exp(s_ref[r] - m_new)
            l_ref[r, 0:1, :] = alpha * l_ref[r, 0:1, :] + jnp.sum(p, axis=0, keepdims=True)
            p_ref[r] = p.astype(BF16)
        for r in range(NSA_REP):
            acc_ref[r] = stats[r][1] * acc_ref[r] + _dot(v, p_ref[r])

    n_far = jnp.maximum(qi - (n_band - 1), 0)

    def far_body(kt, c):
        slc_tile(kt, "far")
        return c

    lax.fori_loop(0, n_far, far_body, 0)
    for r in range(NSA_REP):
        m_ref[r, 0:1, :] = m_ref[r, 0:1, :] + tab_ref[N_BUCKETS - 1, g * NSA_REP + r]

    def band_body(kt, c):
        slc_tile(kt, "band")
        return c

    lax.fori_loop(n_far, qi, band_body, 0)
    slc_tile(qi, "diag")

    n_wt = WINDOW // TK + 1
    outs = []
    for r in range(NSA_REP):
        q = qa_ref[r, 0:HEAD_DIM, :]
        ss, vv, oks = [], [], []
        for c in range(n_wt):
            delta = n_wt - 1 - c
            kt = qi - delta
            ktc = jnp.maximum(kt, 0)
            k0 = pl.multiple_of(ktc * TK, TK)
            s = _dot(kw_ref[0, pl.ds(k0, TK), :], q) + _bias_tile(win_ref, r, delta)
            dist = delta * TK + qry - key
            ok = (dist >= 0) & (dist < WINDOW) & (kt >= 0)
            ss.append(jnp.where(ok, s, NEG))
            oks.append(ok)
            vv.append(vw_ref[ktc])
        mx = jnp.max(ss[0], axis=0, keepdims=True)
        for s in ss[1:]:
            mx = jnp.maximum(mx, jnp.max(s, axis=0, keepdims=True))
        den = jnp.zeros((1, TQ), F32)
        o_w = jnp.zeros((HEAD_DIM, TQ), F32)
        for s, v, ok in zip(ss, vv, oks):
            e = jnp.where(ok, jnp.exp(s - mx), 0.0)
            den = den + jnp.sum(e, axis=0, keepdims=True)
            o_w = o_w + _dot(v, e.astype(BF16))
        o_w = o_w * (1.0 / jnp.maximum(den, 1e-30))
        o_s = acc_ref[r] * (1.0 / jnp.maximum(l_ref[r, 0:1, :], 1e-30))
        o_c = oc_ref[r * HEAD_DIM:(r + 1) * HEAD_DIM, :]
        row0 = 3 * (g * NSA_REP + r)
        gates = [_sigmoid(gate_ref[0, pl.ds(row0 + c, 1), :]) for c in range(3)]
        outs.append(gates[0] * o_c + gates[1] * o_s + gates[2] * o_w)
    o_ref[...] = jnp.concatenate(outs, axis=0).T.astype(o_ref.dtype)


def _gate_weights_t(w_gate):
    pad = HEAD_DIM - w_gate.shape[1]
    return jnp.pad(w_gate.T, ((0, pad), (0, 0))).astype(BF16)


def _key_augment(proj, onehot):
    t_len = proj.shape[1]
    n_sel = t_len // SEL_BLOCK
    pad = LANE - n_sel
    return jnp.concatenate(
        [proj[HN_KSL:HN_KSL + NSA_KV],
         jnp.zeros((NSA_KV, t_len, LANE - HEAD_DIM), BF16),
         jnp.broadcast_to(jnp.pad(onehot, ((0, 0), (0, pad)))[None], (NSA_KV, t_len, LANE))], axis=2)


def _slc_swa(proj, proj_t, mask_t, k_aug, bias_win, rel_bias, gate_t, o_cmp_t):
    t_len = proj.shape[1]
    nt = t_len // TQ
    n_sel = t_len // SEL_BLOCK
    hw = NSA_REP * HEAD_DIM
    return pl.pallas_call(
        _slc_swa_kernel,
        grid=(NSA_KV, nt),
        in_specs=[pl.BlockSpec((1, hw, TQ), lambda g, i: (i, HT_NQ // NSA_REP + g, 0)),
                  pl.BlockSpec((1, n_sel, TQ), lambda g, i: (g, 0, i)),
                  pl.BlockSpec((1, t_len, 2 * LANE), lambda g, i: (g, 0, 0)),
                  pl.BlockSpec((nt, HEAD_DIM, TQ), lambda g, i: (0, HT_VSL + g, 0)),
                  pl.BlockSpec((1, t_len, HEAD_DIM), lambda g, i: (HN_KSW + g, 0, 0)),
                  pl.BlockSpec((nt, HEAD_DIM, TQ), lambda g, i: (0, HT_VSW + g, 0)),
                  pl.BlockSpec((NSA_REP, N_BIAS_SUB, BIAS_SUB, BIAS_SUB), lambda g, i: (g, 0, 0, 0)),
                  pl.BlockSpec(memory_space=pltpu.SMEM),
                  pl.BlockSpec((1, HEAD_DIM, TQ), lambda g, i: (i, 0, 0)),
                  pl.BlockSpec((hw, TQ), lambda g, i: (g, i))],
        out_specs=pl.BlockSpec((TQ, hw), lambda g, i: (i, g)),
        out_shape=jax.ShapeDtypeStruct((t_len, NSA_HEADS * HEAD_DIM), BF16),
        scratch_shapes=[pltpu.VMEM((NSA_REP, 2 * LANE, TQ), BF16),
                        pltpu.VMEM((NSA_REP, SUBLANE, TQ), F32),
                        pltpu.VMEM((NSA_REP, SUBLANE, TQ), F32),
                        pltpu.VMEM((NSA_REP, HEAD_DIM, TQ), F32),
                        pltpu.VMEM((NSA_REP, TK, TQ), F32),
                        pltpu.VMEM((NSA_REP, TK, TQ), BF16)],
        compiler_params=_cparams(("parallel", "arbitrary")),
        name="nsa_slc_swa",
    )(proj_t, mask_t, k_aug, proj_t, proj, proj_t, bias_win, rel_bias, gate_t, o_cmp_t)


def _xattn_kernel(hb_ref, hf_ref, wq_ref, kv_ref, wo_ref, g_ref, b_ref, of_ref, ob_ref):
    q = _dot(hb_ref[...], wq_ref[...]).astype(BF16)
    scale = XA_HEAD_DIM ** -0.5
    outs = []
    for hh in range(XA_HEADS):
        lo = hh * XA_HEAD_DIM
        k = kv_ref[:, lo:lo + XA_HEAD_DIM]
        v = kv_ref[:, XA_W + lo:XA_W + lo + XA_HEAD_DIM]
        s = _dot_nt(q[:, lo:lo + XA_HEAD_DIM], k) * scale
        e = jnp.exp(s - jnp.max(s, axis=1, keepdims=True))
        p = e / jnp.sum(e, axis=1, keepdims=True)
        outs.append(_dot(p.astype(BF16), v).astype(BF16))
    o = jnp.concatenate(outs, axis=1)
    y = _layer_norm(ALPHA * hf_ref[...] + _dot(o, wo_ref[...]), g_ref[...], b_ref[...])
    of_ref[...] = y
    ob_ref[...] = y.astype(BF16)


def _xattn(hb, hf, wq, kv, wo, g, b, tm):
    m, n = hf.shape
    tm = min(tm, m)
    return pl.pallas_call(
        _xattn_kernel,
        grid=(m // tm,),
        in_specs=[pl.BlockSpec((tm, n), lambda i: (i, 0)),
                  pl.BlockSpec((tm, n), lambda i: (i, 0)),
                  pl.BlockSpec((n, XA_W), lambda i: (0, 0)),
                  pl.BlockSpec((N_MEM, 2 * XA_W), lambda i: (0, 0)),
                  pl.BlockSpec((XA_W, n), lambda i: (0, 0)),
                  pl.BlockSpec((1, n), lambda i: (0, 0)),
                  pl.BlockSpec((1, n), lambda i: (0, 0))],
        out_specs=[pl.BlockSpec((tm, n), lambda i: (i, 0)),
                   pl.BlockSpec((tm, n), lambda i: (i, 0))],
        out_shape=[jax.ShapeDtypeStruct((m, n), F32), jax.ShapeDtypeStruct((m, n), BF16)],
        compiler_params=_cparams(("parallel",)),
        name="xattn",
    )(hb, hf, wq, kv, wo, g.reshape(1, n), b.reshape(1, n))


def _ffn_up_kernel(h_ref, wa_ref, wg_ref, cwa_ref, cwg_ref, cba_ref, cbg_ref, o_ref, ta_ref, tg_ref):
    i = pl.program_id(1)

    @pl.when(i == 0)
    def _():
        ta_ref[...] = jnp.zeros_like(ta_ref)
        tg_ref[...] = jnp.zeros_like(tg_ref)

    hb = h_ref[...]
    tm = hb.shape[0]
    row = lax.broadcasted_iota(jnp.int32, (tm, o_ref.shape[1]), 0)

    def conv(w_ref, cw_ref, cb_ref, tail_ref):
        u = _dot(hb, w_ref[...])
        tail = tail_ref[...]
        u1 = jnp.where(row == 0, tail[SUBLANE - 1:SUBLANE, :], pltpu.roll(u, 1, 0))
        u2 = pltpu.roll(u, 2, 0)
        u2 = jnp.where(row == 0, tail[SUBLANE - 2:SUBLANE - 1, :], u2)
        u2 = jnp.where(row == 1, tail[SUBLANE - 1:SUBLANE, :], u2)
        tail_ref[...] = u[tm - SUBLANE:, :]
        return cw_ref[0:1, :] * u2 + cw_ref[1:2, :] * u1 + cw_ref[2:3, :] * u + cb_ref[...]

    a = conv(wa_ref, cwa_ref, cba_ref, ta_ref)
    gg = conv(wg_ref, cwg_ref, cbg_ref, tg_ref)
    o_ref[...] = (a * _gelu(gg)).astype(o_ref.dtype)


def _ffn_up(hb, w_up, conv_w, conv_b, tm, tn):
    m, k = hb.shape
    tm = min(tm, m)
    nj = D_FF // tn
    cb = conv_b.reshape(1, 2 * D_FF)
    return pl.pallas_call(
        _ffn_up_kernel,
        grid=(nj, m // tm),
        in_specs=[pl.BlockSpec((tm, k), lambda j, i: (i, 0)),
                  pl.BlockSpec((k, tn), lambda j, i: (0, j)),
                  pl.BlockSpec((k, tn), lambda j, i: (0, nj + j)),
                  pl.BlockSpec((3, tn), lambda j, i: (0, j)),
                  pl.BlockSpec((3, tn), lambda j, i: (0, nj + j)),
                  pl.BlockSpec((1, tn), lambda j, i: (0, j)),
                  pl.BlockSpec((1, tn), lambda j, i: (0, nj + j))],
        out_specs=pl.BlockSpec((tm, tn), lambda j, i: (i, j)),
        out_shape=jax.ShapeDtypeStruct((m, D_FF), BF16),
        scratch_shapes=[pltpu.VMEM((SUBLANE, tn), F32), pltpu.VMEM((SUBLANE, tn), F32)],
        compiler_params=_cparams(("parallel", "arbitrary")),
        name="ffn_up",
    )(hb, w_up, w_up, conv_w, conv_w, cb, cb)


def _static_tables(t_len):
    n16 = t_len // CMP_STRIDE
    n_sel = t_len // SEL_BLOCK
    n_cmp = (t_len - CMP_LEN) // CMP_STRIDE + 1
    cmp_start = np.arange(n16) * CMP_STRIDE
    cmp_end = cmp_start + CMP_LEN - 1
    sel = np.arange(n_sel)
    overlap_t = ((cmp_start[None, :] < (sel[:, None] + 1) * SEL_BLOCK)
                 & (cmp_end[None, :] >= sel[:, None] * SEL_BLOCK)
                 & (np.arange(n16)[None, :] < n_cmp))
    overlap_cat = np.concatenate([overlap_t, overlap_t], axis=1)
    tri_t = np.arange(TK)[None, :] > np.arange(TK)[:, None]
    tri_cat = np.concatenate([tri_t, tri_t], axis=1)
    onehot = (np.arange(t_len)[:, None] // SEL_BLOCK) == sel[None, :]
    return (jnp.asarray(overlap_cat, BF16), jnp.asarray(tri_cat, BF16), jnp.asarray(onehot, BF16))


def _split_in_proj(w):
    sizes = (S5_CH, 512, 512, 512, 1024, 256, 256, 256, 256, 256, 256, 3 * NSA_HEADS)
    offs = np.cumsum((0,) + sizes)
    u5, sbq, sbk, sbv, nq, kc, vc, ksl, vsl, ksw, vsw, gt = (
        w[:, int(offs[n]):int(offs[n + 1])] for n in range(len(sizes)))
    w_nat = jnp.concatenate([sbk, kc, vc, ksl, ksw], axis=1).astype(BF16)
    w_tr = jnp.concatenate([sbq, nq, sbv, vsl, vsw], axis=1).T.astype(BF16)
    return u5.astype(BF16), w_nat, w_tr, _gate_weights_t(gt)


def kernel(x, mem, w_in, w_out, s5_lambda_re, s5_lambda_im, s5_log_dt, s5_b_re, s5_b_im, s5_c_re, s5_c_im, s5_d, s5_w_glu, s5_b_glu, nsa_cmp_pos, nsa_cmp_w1, nsa_cmp_w2, rel_bias, xa_wq, xa_wkv, xa_wo, ffn_w_up, ffn_conv_w, ffn_conv_b, ffn_w_down, ln_g, ln_b):
    t_len = x.shape[1]
    overlap_cat, tri_cat, onehot = _static_tables(t_len)
    bias_win, bias_cmp = _bias_tables(rel_bias)
    mem_b = mem[0].astype(BF16)
    hf = x[0]
    hb = hf.astype(BF16)
    for l in range(DEPTH):
        w_u5, w_nat, w_tr, w_gate_t = _split_in_proj(w_in[l])
        u5 = _mm(hb, w_u5, F32, 512, 512)
        proj = _mm_heads(hb, w_nat, 512)
        proj_t = _mm_nt(w_tr, hb, BF16, 512)
        gate_t = _mm_nt(w_gate_t, hb, F32, HEAD_DIM)

        s5p = _s5_params(s5_lambda_re[l], s5_lambda_im[l], s5_log_dt[l], s5_b_re[l], s5_b_im[l],
                         s5_c_re[l], s5_c_im[l])
        y_s5 = _s5(u5, s5p, s5_w_glu[l], s5_b_glu[l], s5_d[l])
        y_sb = _stick_breaking(proj, proj_t, tri_cat)

        kvc, kvc_t = _compress(proj, HN_KC, nsa_cmp_pos[l], nsa_cmp_w1[l], nsa_cmp_w2[l])
        o_cmp_t, mask_t = _cmp_select(proj_t, kvc, kvc_t, bias_cmp, rel_bias, overlap_cat)
        y_nsa = _slc_swa(proj, proj_t, mask_t, _key_augment(proj, onehot), bias_win, rel_bias,
                         gate_t, o_cmp_t)

        y = jnp.concatenate([y_s5, y_sb, y_nsa], axis=1)
        hf, hb = _mm_ln(y, w_out[l].astype(BF16), hf, ln_g[l, 0], ln_b[l, 0], 512, 512)

        kv = _mm(mem_b, xa_wkv[l].astype(BF16), BF16, N_MEM, 512)
        hf, hb = _xattn(hb, hf, xa_wq[l].astype(BF16), kv, xa_wo[l].astype(BF16),
                        ln_g[l, 1], ln_b[l, 1], 256)

        act = _ffn_up(hb, ffn_w_up[l].astype(BF16), ffn_conv_w[l], ffn_conv_b[l], 512, 512)
        hf, hb = _mm_ln(act, ffn_w_down[l].astype(BF16), hf, ln_g[l, 2], ln_b[l, 2], 512, 512)
    return hf[None]
```

```python
import functools
import math

import numpy as np
import jax
import jax.numpy as jnp
from jax import lax
from jax.experimental import pallas as pl
from jax.experimental.pallas import tpu as pltpu

F32 = jnp.float32
BF16 = jnp.bfloat16

D_MODEL = 2048
DEPTH = 4
HEAD_DIM = 64
S5_CH = 512
S5_GROUP = 16
S5_NG = 32
S5_P = 64
SB_HEADS = 8
NSA_HEADS = 16
NSA_KV = 4
NSA_REP = 4
CMP_LEN = 32
CMP_STRIDE = 16
CMP_HID = 128
SEL_BLOCK = 64
SEL_TOPN = 16
WINDOW = 512
FORCE_SCORE = 1e4
N_BUCKETS = 32
MAX_DIST = 1024
N_MEM = 256
XA_HEADS = 4
XA_HEAD_DIM = 128
XA_W = 512
D_FF = 5632
ALPHA = (2.0 * DEPTH) ** 0.25
LN_EPS = 1e-5
NEG = -1e30

LANE = 128
SUBLANE = 8
MXU_DIM = 256
VMEM_LIMIT = 56 * 1024 * 1024
TQ = 512
TK = 256
SUB_Q = TQ // LANE
SUB_K = TK // LANE
HEADS_PER_STEP = 4
S5_SEG = 64
S5_CHUNK = SUBLANE * S5_SEG
S5_CB = 8
BIAS_SUB = LANE
FAR_SUB = MAX_DIST // BIAS_SUB + 1
N_BIAS_SUB = FAR_SUB + SUB_K + SUB_Q - 2

HT_SBQ = 0
HT_NQ = HT_SBQ + SB_HEADS
HT_Q_ROWS = (HT_NQ + NSA_HEADS) * HEAD_DIM
HT_SBV = 0
HT_VSL = HT_SBV + SB_HEADS
HT_VSW = HT_VSL + NSA_KV
HT_V_ROWS = (HT_VSW + NSA_KV) * HEAD_DIM
HN_SBK = 0
HN_KC = HN_SBK + SB_HEADS
HN_KSL = HN_KC + 2 * NSA_KV
HN_KSW = HN_KSL + NSA_KV
HN_HEADS = HN_KSW + NSA_KV


def _cparams(sem):
    return pltpu.CompilerParams(dimension_semantics=sem, vmem_limit_bytes=VMEM_LIMIT)


def _dot(a, b):
    return jnp.dot(a, b, preferred_element_type=F32)


def _dot_nt(a, b):
    return lax.dot_general(a, b, (((1,), (1,)), ((), ())), preferred_element_type=F32)


def _split_bf16(x):
    hi = x.astype(BF16)
    lo = (x - hi.astype(F32)).astype(BF16)
    return hi, lo


def _gelu(x):
    c = math.sqrt(2.0 / math.pi)
    return 0.5 * x * (1.0 + jnp.tanh(c * (x + 0.044715 * (x * x * x))))


def _sigmoid(x):
    return 1.0 / (1.0 + jnp.exp(-x))


def _layer_norm(x, g, b):
    mu = jnp.mean(x, axis=-1, keepdims=True)
    xc = x - mu
    var = jnp.mean(xc * xc, axis=-1, keepdims=True)
    return xc * lax.rsqrt(var + LN_EPS) * g + b


def _mm_kernel(a_ref, b_ref, o_ref):
    o_ref[...] = _dot(a_ref[...], b_ref[...]).astype(o_ref.dtype)


def _mm(a, b, out_dtype, tm, tn):
    m, k = a.shape
    n = b.shape[1]
    tm = min(tm, m)
    return pl.pallas_call(
        _mm_kernel,
        grid=(n // tn, m // tm),
        in_specs=[pl.BlockSpec((tm, k), lambda j, i: (i, 0)),
                  pl.BlockSpec((k, tn), lambda j, i: (0, j))],
        out_specs=pl.BlockSpec((tm, tn), lambda j, i: (i, j)),
        out_shape=jax.ShapeDtypeStruct((m, n), out_dtype),
        compiler_params=_cparams(("parallel", "arbitrary")),
        name="mm",
    )(a, b)


def _mm_heads_kernel(a_ref, b_ref, o_ref):
    o = _dot(a_ref[...], b_ref[...]).astype(o_ref.dtype)
    for r in range(o_ref.shape[0]):
        o_ref[r] = o[:, r * HEAD_DIM:(r + 1) * HEAD_DIM]


def _mm_heads(a, b, tm):
    m, k = a.shape
    n = b.shape[1]
    tn = MXU_DIM
    hpt = tn // HEAD_DIM
    tm = min(tm, m)
    return pl.pallas_call(
        _mm_heads_kernel,
        grid=(n // tn, m // tm),
        in_specs=[pl.BlockSpec((tm, k), lambda j, i: (i, 0)),
                  pl.BlockSpec((k, tn), lambda j, i: (0, j))],
        out_specs=pl.BlockSpec((hpt, tm, HEAD_DIM), lambda j, i: (j, i, 0)),
        out_shape=jax.ShapeDtypeStruct((n // HEAD_DIM, m, HEAD_DIM), BF16),
        compiler_params=_cparams(("parallel", "arbitrary")),
        name="mm_heads",
    )(a, b)


def _mm_nt_kernel(w_ref, a_ref, o_ref):
    o_ref[0] = _dot_nt(w_ref[...], a_ref[...]).astype(o_ref.dtype)


def _mm_nt(w_t, a, out_dtype, tn, tm):
    n, k = w_t.shape
    m = a.shape[0]
    return pl.pallas_call(
        _mm_nt_kernel,
        grid=(n // tn, m // tm),
        in_specs=[pl.BlockSpec((tn, k), lambda j, i: (j, 0)),
                  pl.BlockSpec((tm, k), lambda j, i: (i, 0))],
        out_specs=pl.BlockSpec((1, tn, tm), lambda j, i: (i, j, 0)),
        out_shape=jax.ShapeDtypeStruct((m // tm, n, tm), out_dtype),
        compiler_params=_cparams(("parallel", "arbitrary")),
        name="mm_nt",
    )(w_t, a)


def _mm_ln_kernel(a_ref, w_ref, h_ref, g_ref, b_ref, of_ref, ob_ref, acc_ref):
    kk = pl.program_id(1)

    @pl.when(kk == 0)
    def _():
        acc_ref[...] = jnp.zeros_like(acc_ref)

    acc_ref[...] += _dot(a_ref[...], w_ref[...])

    @pl.when(kk == pl.num_programs(1) - 1)
    def _():
        y = _layer_norm(ALPHA * h_ref[...] + acc_ref[...], g_ref[...], b_ref[...])
        of_ref[...] = y
        ob_ref[...] = y.astype(BF16)


def _mm_ln(a, w, h, g, b, tm, tk):
    m, k = a.shape
    n = w.shape[1]
    tm = min(tm, m)
    return pl.pallas_call(
        _mm_ln_kernel,
        grid=(m // tm, k // tk),
        in_specs=[pl.BlockSpec((tm, tk), lambda i, kk: (i, kk)),
                  pl.BlockSpec((tk, n), lambda i, kk: (kk, 0)),
                  pl.BlockSpec((tm, n), lambda i, kk: (i, 0)),
                  pl.BlockSpec((1, n), lambda i, kk: (0, 0)),
                  pl.BlockSpec((1, n), lambda i, kk: (0, 0))],
        out_specs=[pl.BlockSpec((tm, n), lambda i, kk: (i, 0)),
                   pl.BlockSpec((tm, n), lambda i, kk: (i, 0))],
        out_shape=[jax.ShapeDtypeStruct((m, n), F32), jax.ShapeDtypeStruct((m, n), BF16)],
        scratch_shapes=[pltpu.VMEM((tm, n), F32)],
        compiler_params=_cparams(("parallel", "arbitrary")),
        name="mm_ln",
    )(a, w, h, g.reshape(1, n), b.reshape(1, n))


def _s5_kernel(u_ref, bm_ref, cm_ref, a_ref, d_ref, z_ref, s_ref, p_ref, carry_ref):
    tc = pl.program_id(1)
    w = s_ref.shape[1] // 2
    n_step = p_ref.shape[0]
    ar1 = a_ref[0:1, :]
    ai1 = a_ref[1:2, :]

    @pl.when(tc == 0)
    def _():
        carry_ref[...] = jnp.zeros_like(carry_ref)
        p_ref[0:1, 0:w] = ar1
        p_ref[0:1, w:2 * w] = ai1

        def pw(t, c):
            pr, pi = c
            nr = pr * ar1 - pi * ai1
            ni = pr * ai1 + pi * ar1
            p_ref[pl.ds(t, 1), 0:w] = nr
            p_ref[pl.ds(t, 1), w:2 * w] = ni
            return nr, ni

        lax.fori_loop(1, n_step, pw, (ar1, ai1))

    u = u_ref[...]
    s_ref[...] = _dot(u.astype(BF16), bm_ref[0])

    ar = jnp.broadcast_to(ar1, (SUBLANE, w))
    ai = jnp.broadcast_to(ai1, (SUBLANE, w))

    def step(t, c):
        sr, si = c
        r0 = pl.multiple_of(t * SUBLANE, SUBLANE)
        xr = s_ref[pl.ds(r0, SUBLANE), 0:w]
        xi = s_ref[pl.ds(r0, SUBLANE), w:2 * w]
        nr = ar * sr - ai * si + xr
        ni = ar * si + ai * sr + xi
        s_ref[pl.ds(r0, SUBLANE), 0:w] = nr
        s_ref[pl.ds(r0, SUBLANE), w:2 * w] = ni
        return nr, ni

    zero = jnp.zeros((SUBLANE, w), F32)
    fr, fi = lax.fori_loop(0, n_step, step, (zero, zero), unroll=8)

    amr = p_ref[n_step - 1:n_step, 0:w]
    ami = p_ref[n_step - 1:n_step, w:2 * w]
    row = lax.broadcasted_iota(jnp.int32, (SUBLANE, w), 0)
    cr = carry_ref[0:1, 0:w]
    ci = carry_ref[0:1, w:2 * w]
    cin_r = jnp.zeros((SUBLANE, w), F32)
    cin_i = jnp.zeros((SUBLANE, w), F32)
    for seg in range(SUBLANE):
        cin_r = jnp.where(row == seg, cr, cin_r)
        cin_i = jnp.where(row == seg, ci, cin_i)
        nr = fr[seg:seg + 1, :] + amr * cr - ami * ci
        ni = fi[seg:seg + 1, :] + amr * ci + ami * cr
        cr, ci = nr, ni
    carry_ref[0:1, 0:w] = cr
    carry_ref[0:1, w:2 * w] = ci

    def fix(t, c):
        r0 = pl.multiple_of(t * SUBLANE, SUBLANE)
        pr = p_ref[pl.ds(t, 1), 0:w]
        pi = p_ref[pl.ds(t, 1), w:2 * w]
        s_ref[pl.ds(r0, SUBLANE), 0:w] += pr * cin_r - pi * cin_i
        s_ref[pl.ds(r0, SUBLANE), w:2 * w] += pr * cin_i + pi * cin_r
        return c

    lax.fori_loop(0, n_step, fix, 0, unroll=8)

    y = _dot(s_ref[...].astype(BF16), cm_ref[0]) + d_ref[...] * u
    z_ref[...] = _gelu(y)


def _s5_glu_kernel(z_ref, w_ref, b_ref, o_ref):
    z = z_ref[...]
    gate = _sigmoid(_dot(z.astype(BF16), w_ref[...]) + b_ref[...])
    o_ref[...] = (z * gate).astype(o_ref.dtype)


def _s5_params(lam_re, lam_im, log_dt, b_re, b_im, c_re, c_im):
    delta = jnp.exp(log_dt)[:, None]
    mag = jnp.exp(lam_re * delta)
    ar = mag * jnp.cos(lam_im * delta)
    ai = mag * jnp.sin(lam_im * delta)
    den = lam_re * lam_re + lam_im * lam_im
    cr = ((ar - 1.0) * lam_re + ai * lam_im) / den
    ci = (ai * lam_re - (ar - 1.0) * lam_im) / den
    br = cr[..., None] * b_re - ci[..., None] * b_im
    bi = cr[..., None] * b_im + ci[..., None] * b_re
    ncb = S5_NG // S5_CB
    eye = jnp.eye(S5_CB, dtype=F32)

    def in_mat(b):
        b = b.reshape(ncb, S5_CB, S5_P, S5_GROUP)
        m = jnp.einsum('ngpc,gh->ngchp', b, eye)
        return m.reshape(ncb, S5_CB * S5_GROUP, S5_CB * S5_P)

    def out_mat(c):
        c = c.reshape(ncb, S5_CB, S5_GROUP, S5_P)
        m = jnp.einsum('ngcp,gh->ngphc', c, eye)
        return m.reshape(ncb, S5_CB * S5_P, S5_CB * S5_GROUP)

    bm = jnp.concatenate([in_mat(br), in_mat(bi)], axis=2).astype(BF16)
    cm = jnp.concatenate([out_mat(c_re), -out_mat(c_im)], axis=1).astype(BF16)
    a = jnp.stack([ar.reshape(-1), ai.reshape(-1)], axis=0)
    return bm, cm, a


def _s5(u, p, w_glu, b_glu, d):
    t_len = u.shape[0]
    nc = t_len // S5_CHUNK
    ncb = S5_NG // S5_CB
    wl = S5_CB * S5_GROUP
    ws = S5_CB * S5_P
    bm, cm, a = p
    up = u.reshape(nc, SUBLANE, S5_SEG, S5_CH).transpose(0, 2, 1, 3).reshape(t_len, S5_CH)
    z = pl.pallas_call(
        _s5_kernel,
        grid=(ncb, nc),
        in_specs=[pl.BlockSpec((S5_CHUNK, wl), lambda c, t: (t, c)),
                  pl.BlockSpec((1, wl, 2 * ws), lambda c, t: (c, 0, 0)),
                  pl.BlockSpec((1, 2 * ws, wl), lambda c, t: (c, 0, 0)),
                  pl.BlockSpec((2, ws), lambda c, t: (0, c)),
                  pl.BlockSpec((1, wl), lambda c, t: (0, c))],
        out_specs=pl.BlockSpec((S5_CHUNK, wl), lambda c, t: (t, c)),
        out_shape=jax.ShapeDtypeStruct((t_len, S5_CH), F32),
        scratch_shapes=[pltpu.VMEM((S5_CHUNK, 2 * ws), F32),
                        pltpu.VMEM((S5_SEG, 2 * ws), F32),
                        pltpu.VMEM((SUBLANE, 2 * ws), F32)],
        compiler_params=_cparams(("parallel", "arbitrary")),
        name="s5_scan",
    )(up, bm, cm, a, d.reshape(1, S5_CH))
    tm = min(1024, t_len)
    y = pl.pallas_call(
        _s5_glu_kernel,
        grid=(t_len // tm,),
        in_specs=[pl.BlockSpec((tm, S5_CH), lambda i: (i, 0)),
                  pl.BlockSpec((S5_CH, S5_CH), lambda i: (0, 0)),
                  pl.BlockSpec((1, S5_CH), lambda i: (0, 0))],
        out_specs=pl.BlockSpec((tm, S5_CH), lambda i: (i, 0)),
        out_shape=jax.ShapeDtypeStruct((t_len, S5_CH), BF16),
        compiler_params=_cparams(("parallel",)),
        name="s5_glu",
    )(z, w_glu.astype(BF16), b_glu.reshape(1, S5_CH))
    return y.reshape(nc, S5_SEG, SUBLANE, S5_CH).transpose(0, 2, 1, 3).reshape(t_len, S5_CH)


def _sb_kernel(q_ref, k_ref, v_ref, u_ref, o_ref, r_ref, acc_ref, nl_ref, arg_ref, w_ref):
    qi = pl.program_id(1)
    ucat = u_ref[...]
    qs = [q_ref[0, r * HEAD_DIM:(r + 1) * HEAD_DIM, :] * 0.125 for r in range(HEADS_PER_STEP)]
    r_ref[...] = jnp.zeros_like(r_ref)
    acc_ref[...] = jnp.zeros_like(acc_ref)
    key = lax.broadcasted_iota(jnp.int32, (TK, TQ), 0)
    qry = lax.broadcasted_iota(jnp.int32, (TK, TQ), 1)

    def tile(kt, diag_off):
        k0 = pl.multiple_of(kt * TK, TK)
        mask = None if diag_off is None else (key + diag_off) < qry
        sums = []
        for r in range(HEADS_PER_STEP):
            z = _dot(k_ref[r, pl.ds(k0, TK), :], qs[r])
            nl = jnp.maximum(z, 0.0) + jnp.log(1.0 + jnp.exp(-jnp.abs(z)))
            arg_ref[r] = z - nl
            if mask is not None:
                nl = jnp.where(mask, nl, 0.0)
            hi, lo = _split_bf16(nl)
            nl_ref[r, 0:TK, :] = hi
            nl_ref[r, TK:2 * TK, :] = lo
            sums.append(jnp.sum(nl, axis=0, keepdims=True))
        for r in range(HEADS_PER_STEP):
            later = _dot(ucat, nl_ref[r])
            w = jnp.exp(arg_ref[r] - later)
            if mask is not None:
                w = jnp.where(mask, w, 0.0)
            w_ref[r] = w.astype(BF16)
        for r in range(HEADS_PER_STEP):
            v = v_ref[kt, r * HEAD_DIM:(r + 1) * HEAD_DIM, :]
            r_old = r_ref[r, 0:1, :]
            acc_ref[r] += _dot(v, w_ref[r]) * jnp.exp(-r_old)
            r_ref[r, 0:1, :] = r_old + sums[r]

    n_diag = TQ // TK
    for c in reversed(range(n_diag)):
        tile(n_diag * qi + c, c * TK)

    def alive():
        return (jnp.max(jnp.exp(-r_ref[:, 0:1, :])) > 0.0).astype(jnp.int32)

    def cond(state):
        return (state[0] >= 0) & (state[1] > 0)

    def body(state):
        tile(state[0], None)
        return state[0] - 1, alive()

    lax.while_loop(cond, body, (n_diag * qi - 1, alive()))
    o_t = jnp.concatenate([acc_ref[r] for r in range(HEADS_PER_STEP)], axis=0)
    o_ref[...] = o_t.T.astype(o_ref.dtype)


def _stick_breaking(proj, q_t, v_t, ucat):
    t_len = proj.shape[1]
    nt = t_len // TQ
    hw = HEADS_PER_STEP * HEAD_DIM
    return pl.pallas_call(
        _sb_kernel,
        grid=(SB_HEADS // HEADS_PER_STEP, nt),
        in_specs=[pl.BlockSpec((1, hw, TQ), lambda h, i: (i, HT_SBQ // HEADS_PER_STEP + h, 0)),
                  pl.BlockSpec((HEADS_PER_STEP, t_len, HEAD_DIM),
                               lambda h, i: (HN_SBK // HEADS_PER_STEP + h, 0, 0)),
                  pl.BlockSpec((t_len // TK, hw, TK),
                               lambda h, i: (0, HT_SBV // HEADS_PER_STEP + h, 0)),
                  pl.BlockSpec((TK, 2 * TK), lambda h, i: (0, 0))],
        out_specs=pl.BlockSpec((TQ, hw), lambda h, i: (i, h)),
        out_shape=jax.ShapeDtypeStruct((t_len, SB_HEADS * HEAD_DIM), BF16),
        scratch_shapes=[pltpu.VMEM((HEADS_PER_STEP, SUBLANE, TQ), F32),
                        pltpu.VMEM((HEADS_PER_STEP, HEAD_DIM, TQ), F32),
                        pltpu.VMEM((HEADS_PER_STEP, 2 * TK, TQ), BF16),
                        pltpu.VMEM((HEADS_PER_STEP, TK, TQ), F32),
                        pltpu.VMEM((HEADS_PER_STEP, TK, TQ), BF16)],
        compiler_params=_cparams(("parallel", "arbitrary")),
        name="stick_breaking",
    )(q_t, proj, v_t, ucat)


def _t5_bucket(dist):
    n = jnp.maximum(dist, 0)
    max_exact = N_BUCKETS // 2
    nf = jnp.maximum(n, 1).astype(jnp.float32)
    large = max_exact + (jnp.log(nf / max_exact) / math.log(MAX_DIST / max_exact)
                         * (N_BUCKETS - max_exact)).astype(jnp.int32)
    large = jnp.minimum(large, N_BUCKETS - 1)
    return jnp.where(n < max_exact, n, large)


def _bias_lookup_kernel(tab_ref, idx_ref, o_ref):
    h = pl.program_id(0)
    idx = idx_ref[...]
    acc = jnp.zeros(idx.shape, F32)
    for b in range(N_BUCKETS):
        acc = jnp.where(idx == b, tab_ref[b, h], acc)
    o_ref[0] = acc


def _bias_lookup(tab, idx):
    r, c = idx.shape
    n_h = tab.shape[1]
    return pl.pallas_call(
        _bias_lookup_kernel,
        grid=(n_h,),
        in_specs=[pl.BlockSpec(memory_space=pltpu.SMEM),
                  pl.BlockSpec((r, c), lambda h: (0, 0))],
        out_specs=pl.BlockSpec((1, r, c), lambda h: (h, 0, 0)),
        out_shape=jax.ShapeDtypeStruct((n_h, r, c), F32),
        compiler_params=_cparams(("arbitrary",)),
        name="bias_lookup",
    )(tab, idx)


def _bias_tables(rel_bias):
    i = jnp.arange(BIAS_SUB)
    d_win = (BIAS_SUB * jnp.arange(N_BIAS_SUB)[:, None, None] + i[None, None, :] - i[None, :, None])
    idx_win = _t5_bucket(d_win).reshape(N_BIAS_SUB * BIAS_SUB, BIAS_SUB)
    win = _bias_lookup(rel_bias, idx_win).reshape(NSA_HEADS, N_BIAS_SUB, BIAS_SUB, BIAS_SUB)
    n_m = (LANE * CMP_STRIDE) // TQ
    m = jnp.arange(n_m)[:, None, None]
    c = jnp.arange(2 * LANE)[None, :, None]
    qi = jnp.arange(TQ)[None, None, :]
    d_cmp = TQ * m + qi + LANE * CMP_STRIDE - CMP_STRIDE * c - (CMP_LEN - 1)
    idx_cmp = _t5_bucket(d_cmp).reshape(n_m * 2 * LANE, TQ)
    cmp_t = _bias_lookup(rel_bias, idx_cmp).reshape(NSA_HEADS, n_m, 2 * LANE, TQ)
    return win, cmp_t


def _compress_kernel(c0_ref, c1_ref, w1_ref, pos_ref, w2_ref, w2t_ref, o_ref, ot_ref):
    half = c0_ref.shape[3]
    w1 = w1_ref[0]
    posb = _dot(pos_ref[0], w1)[0:1, :]
    hid = _dot(c0_ref[0, 0], w1[0:half, :]) + _dot(c1_ref[0, 0], w1[half:, :]) + posb
    act = _gelu(hid).astype(BF16)
    o_ref[0, 0] = _dot(act, w2_ref[0]).astype(o_ref.dtype)
    ot_ref[0, 0] = _dot_nt(w2t_ref[0], act).astype(ot_ref.dtype)


def _compress(proj, head0, cmp_pos, cmp_w1, cmp_w2):
    t_len = proj.shape[1]
    n16 = t_len // CMP_STRIDE
    half = CMP_STRIDE * HEAD_DIM
    kv = proj[head0:head0 + 2 * NSA_KV].reshape(2, NSA_KV, n16, half)
    kv_next = jnp.concatenate([kv[:, :, 1:], jnp.zeros((2, NSA_KV, 1, half), BF16)], axis=2)
    pos = jnp.broadcast_to(cmp_pos.reshape(2, 1, CMP_LEN * HEAD_DIM), (2, SUBLANE, CMP_LEN * HEAD_DIM))
    w2 = cmp_w2.astype(BF16)
    return pl.pallas_call(
        _compress_kernel,
        grid=(2, NSA_KV),
        in_specs=[pl.BlockSpec((1, 1, n16, half), lambda j, g: (j, g, 0, 0)),
                  pl.BlockSpec((1, 1, n16, half), lambda j, g: (j, g, 0, 0)),
                  pl.BlockSpec((1, 2 * half, CMP_HID), lambda j, g: (j, 0, 0)),
                  pl.BlockSpec((1, SUBLANE, 2 * half), lambda j, g: (j, 0, 0)),
                  pl.BlockSpec((1, CMP_HID, HEAD_DIM), lambda j, g: (j, 0, 0)),
                  pl.BlockSpec((1, HEAD_DIM, CMP_HID), lambda j, g: (j, 0, 0))],
        out_specs=[pl.BlockSpec((1, 1, n16, HEAD_DIM), lambda j, g: (j, g, 0, 0)),
                   pl.BlockSpec((1, 1, HEAD_DIM, n16), lambda j, g: (j, g, 0, 0))],
        out_shape=[jax.ShapeDtypeStruct((2, NSA_KV, n16, HEAD_DIM), BF16),
                   jax.ShapeDtypeStruct((2, NSA_KV, HEAD_DIM, n16), BF16)],
        compiler_params=_cparams(("parallel", "arbitrary")),
        name="nsa_compress",
    )(kv, kv_next, cmp_w1.astype(BF16), pos.astype(BF16), w2, w2.transpose(0, 2, 1))


def _cmp_select_kernel(q_ref, kc_ref, vct_ref, bias_ref, tab_ref, ov_ref, oc_ref, mn_ref, *, n_cmp):
    g = pl.program_id(0)
    qi = pl.program_id(1)
    n16 = kc_ref.shape[2]
    n_ct = n16 // LANE
    kc = kc_ref[0, 0]
    vct = vct_ref[0, 0]
    q_per_ct = (LANE * CMP_STRIDE) // TQ
    jd = qi // q_per_ct
    n = lax.broadcasted_iota(jnp.int32, (n16, TQ), 0)
    t = qi * TQ + lax.broadcasted_iota(jnp.int32, (n16, TQ), 1)
    valid = (CMP_STRIDE * n + (CMP_LEN - 1) <= t) & (n < n_cmp)
    psum = jnp.zeros((n16, TQ), F32)
    for r in range(NSA_REP):
        q = q_ref[0, r * HEAD_DIM:(r + 1) * HEAD_DIM, :] * 0.125
        far = tab_ref[N_BUCKETS - 1, g * NSA_REP + r]
        rows = []
        for jc in range(n_ct):
            near = jnp.where(jc == jd, bias_ref[r, 0, LANE:2 * LANE, :], bias_ref[r, 0, 0:LANE, :])
            rows.append(jnp.where((jc == jd) | (jc == jd - 1), near, far))
        bias = rows[0] if n_ct == 1 else jnp.concatenate(rows, axis=0)
        s = jnp.where(valid, _dot(kc, q) + bias, NEG)
        mx = jnp.max(s, axis=0, keepdims=True)
        e = jnp.where(valid, jnp.exp(s - mx), 0.0)
        p = e * (1.0 / jnp.maximum(jnp.sum(e, axis=0, keepdims=True), 1e-30))
        psum = psum + p
        oc_ref[r * HEAD_DIM:(r + 1) * HEAD_DIM, :] = _dot(vct, p.astype(BF16))
    hi, lo = _split_bf16(psum)
    imp = _dot(ov_ref[...], jnp.concatenate([hi, lo], axis=0))
    n_sel = imp.shape[0]
    j = lax.broadcasted_iota(jnp.int32, (n_sel, LANE), 0)
    jf = j.astype(F32)
    for cc in range(TQ // LANE):
        tq = qi * TQ + cc * LANE + lax.broadcasted_iota(jnp.int32, (n_sel, LANE), 1)
        cur = tq // SEL_BLOCK
        forced = (j == 0) | (j == cur) | (j == cur - 1)
        score = jnp.where(j <= cur,
                          jnp.where(forced, FORCE_SCORE, imp[:, cc * LANE:(cc + 1) * LANE]), -1.0)
        sel = jnp.zeros((n_sel, LANE), F32)
        for _ in range(min(SEL_TOPN, n_sel)):
            mx = jnp.max(score, axis=0, keepdims=True)
            first = jnp.min(jnp.where(score == mx, jf, float(n_sel)), axis=0, keepdims=True)
            hit = jf == first
            sel = jnp.where(hit, 1.0, sel)
            score = jnp.where(hit, -jnp.inf, score)
        mn_ref[0, :, cc * LANE:(cc + 1) * LANE] = (sel - 1.0).astype(mn_ref.dtype)


def _cmp_select(q_t, kc, vct, bias_cmp, rel_bias, overlap_cat):
    nt = q_t.shape[0]
    t_len = nt * TQ
    n16 = kc.shape[2]
    n_sel = t_len // SEL_BLOCK
    n_cmp = (t_len - CMP_LEN) // CMP_STRIDE + 1
    q_per_ct = (LANE * CMP_STRIDE) // TQ
    hw = NSA_REP * HEAD_DIM
    return pl.pallas_call(
        functools.partial(_cmp_select_kernel, n_cmp=n_cmp),
        grid=(NSA_KV, nt),
        in_specs=[pl.BlockSpec((1, hw, TQ), lambda g, i: (i, HT_NQ // NSA_REP + g, 0)),
                  pl.BlockSpec((1, 1, n16, HEAD_DIM), lambda g, i: (0, g, 0, 0)),
                  pl.BlockSpec((1, 1, HEAD_DIM, n16), lambda g, i: (1, g, 0, 0)),
                  pl.BlockSpec((NSA_REP, 1, 2 * LANE, TQ), lambda g, i: (g, i % q_per_ct, 0, 0)),
                  pl.BlockSpec(memory_space=pltpu.SMEM),
                  pl.BlockSpec((n_sel, 2 * n16), lambda g, i: (0, 0))],
        out_specs=[pl.BlockSpec((hw, TQ), lambda g, i: (g, i)),
                   pl.BlockSpec((1, n_sel, TQ), lambda g, i: (g, 0, i))],
        out_shape=[jax.ShapeDtypeStruct((NSA_HEADS * HEAD_DIM, t_len), F32),
                   jax.ShapeDtypeStruct((NSA_KV, n_sel, t_len), BF16)],
        compiler_params=_cparams(("parallel", "arbitrary")),
        name="nsa_cmp_select",
    )(q_t, kc, vct, bias_cmp, rel_bias, overlap_cat)


def _bias_tile(win_ref, r, base):
    rows = []
    for b in range(SUB_K):
        cols = []
        for a in range(SUB_Q):
            idx = jnp.maximum(base + a - b, 0)
            cols.append(win_ref[r, idx])
        rows.append(jnp.concatenate(cols, axis=1))
    return jnp.concatenate(rows, axis=0)


def _slc_swa_kernel(q_ref, mn_ref, ka_ref, vs_ref, kw_ref, vw_ref, win_ref, tab_ref, gate_ref,
                    oc_ref, o_ref, qa_ref, m_ref, l_ref, acc_ref, s_ref, p_ref):
    g = pl.program_id(0)
    qi = pl.program_id(1)
    n_sel = mn_ref.shape[1]
    n_diag = TQ // TK
    big = mn_ref[0] * jnp.asarray(1e30, BF16)
    for r in range(NSA_REP):
        qa_ref[r, 0:HEAD_DIM, :] = q_ref[0, r * HEAD_DIM:(r + 1) * HEAD_DIM, :] * 0.125
        qa_ref[r, HEAD_DIM:LANE, :] = jnp.zeros((LANE - HEAD_DIM, TQ), BF16)
        qa_ref[r, LANE:LANE + n_sel, :] = big
        if n_sel < LANE:
            qa_ref[r, LANE + n_sel:2 * LANE, :] = jnp.zeros((LANE - n_sel, TQ), BF16)
    m_ref[...] = jnp.full_like(m_ref, NEG)
    l_ref[...] = jnp.zeros_like(l_ref)
    acc_ref[...] = jnp.zeros_like(acc_ref)
    key = lax.broadcasted_iota(jnp.int32, (TK, TQ), 0)
    qry = lax.broadcasted_iota(jnp.int32, (TK, TQ), 1)

    def slc_tile(kt, far, diag_off):
        k0 = pl.multiple_of(kt * TK, TK)
        ka = ka_ref[0, pl.ds(k0, TK), :]
        v = vs_ref[kt]
        stats = []
        for r in range(NSA_REP):
            s = _dot(ka, qa_ref[r])
            if not far:
                s = s + _bias_tile(win_ref, r, SUB_Q * qi - SUB_K * kt)
            if diag_off is not None:
                s = jnp.where(key + diag_off <= qry, s, NEG)
            s_ref[r] = s
            m_old = m_ref[r, 0:1, :]
            m_new = jnp.maximum(m_old, jnp.max(s, axis=0, keepdims=True))
            m_ref[r, 0:1, :] = m_new
            stats.append((m_new, jnp.exp(m_old - m_new)))
        for r in range(NSA_REP):
            m_new, alpha = stats[r]
            p = jnp.exp(s_ref[r] - m_new)
            l_ref[r, 0:1, :] = alpha * l_ref[r, 0:1, :] + jnp.sum(p, axis=0, keepdims=True)
            p_ref[r] = p.astype(BF16)
        for r in range(NSA_REP):
            acc_ref[r] = stats[r][1] * acc_ref[r] + _dot(v, p_ref[r])

    n_far = jnp.maximum((SUB_Q * qi - (SUB_K - 1) - FAR_SUB) // SUB_K + 1, 0)

    def far_body(kt, c):
        slc_tile(kt, True, None)
        return c

    lax.fori_loop(0, n_far, far_body, 0)
    for r in range(NSA_REP):
        m_ref[r, 0:1, :] = m_ref[r, 0:1, :] + tab_ref[N_BUCKETS - 1, g * NSA_REP + r]

    def band_body(kt, c):
        slc_tile(kt, False, None)
        return c

    lax.fori_loop(n_far, n_diag * qi, band_body, 0)
    for c in range(n_diag):
        slc_tile(n_diag * qi + c, False, c * TK)

    n_wt = WINDOW // TK + n_diag
    outs = []
    for r in range(NSA_REP):
        q = qa_ref[r, 0:HEAD_DIM, :]
        ss, vv, oks = [], [], []
        for c in range(n_wt):
            back = WINDOW // TK - c
            kt = n_diag * qi - back
            ktc = jnp.maximum(kt, 0)
            k0 = pl.multiple_of(ktc * TK, TK)
            s = _dot(kw_ref[0, pl.ds(k0, TK), :], q) + _bias_tile(win_ref, r, SUB_K * back)
            dist = back * TK + qry - key
            ok = (dist >= 0) & (dist < WINDOW) & (kt >= 0)
            ss.append(jnp.where(ok, s, NEG))
            oks.append(ok)
            vv.append(vw_ref[ktc])
        mx = jnp.max(ss[0], axis=0, keepdims=True)
        for s in ss[1:]:
            mx = jnp.maximum(mx, jnp.max(s, axis=0, keepdims=True))
        den = jnp.zeros((1, TQ), F32)
        o_w = jnp.zeros((HEAD_DIM, TQ), F32)
        for s, v, ok in zip(ss, vv, oks):
            e = jnp.where(ok, jnp.exp(s - mx), 0.0)
            den = den + jnp.sum(e, axis=0, keepdims=True)
            o_w = o_w + _dot(v, e.astype(BF16))
        o_w = o_w * (1.0 / jnp.maximum(den, 1e-30))
        o_s = acc_ref[r] * (1.0 / jnp.maximum(l_ref[r, 0:1, :], 1e-30))
        o_c = oc_ref[r * HEAD_DIM:(r + 1) * HEAD_DIM, :]
        row0 = 3 * (g * NSA_REP + r)
        gates = [_sigmoid(gate_ref[0, pl.ds(row0 + c, 1), :]) for c in range(3)]
        outs.append(gates[0] * o_c + gates[1] * o_s + gates[2] * o_w)
    o_ref[...] = jnp.concatenate(outs, axis=0).T.astype(o_ref.dtype)


def _gate_weights_t(w_gate):
    pad = HEAD_DIM - w_gate.shape[1]
    return jnp.pad(w_gate.T, ((0, pad), (0, 0))).astype(BF16)


def _key_augment(proj, onehot):
    t_len = proj.shape[1]
    n_sel = t_len // SEL_BLOCK
    pad = LANE - n_sel
    return jnp.concatenate(
        [proj[HN_KSL:HN_KSL + NSA_KV],
         jnp.zeros((NSA_KV, t_len, LANE - HEAD_DIM), BF16),
         jnp.broadcast_to(jnp.pad(onehot, ((0, 0), (0, pad)))[None], (NSA_KV, t_len, LANE))], axis=2)


def _slc_swa(proj, q_t, v_t, mask_t, k_aug, bias_win, rel_bias, gate_t, o_cmp_t):
    t_len = proj.shape[1]
    nt = t_len // TQ
    nkt = t_len // TK
    n_sel = t_len // SEL_BLOCK
    hw = NSA_REP * HEAD_DIM
    return pl.pallas_call(
        _slc_swa_kernel,
        grid=(NSA_KV, nt),
        in_specs=[pl.BlockSpec((1, hw, TQ), lambda g, i: (i, HT_NQ // NSA_REP + g, 0)),
                  pl.BlockSpec((1, n_sel, TQ), lambda g, i: (g, 0, i)),
                  pl.BlockSpec((1, t_len, 2 * LANE), lambda g, i: (g, 0, 0)),
                  pl.BlockSpec((nkt, HEAD_DIM, TK), lambda g, i: (0, HT_VSL + g, 0)),
                  pl.BlockSpec((1, t_len, HEAD_DIM), lambda g, i: (HN_KSW + g, 0, 0)),
                  pl.BlockSpec((nkt, HEAD_DIM, TK), lambda g, i: (0, HT_VSW + g, 0)),
                  pl.BlockSpec((NSA_REP, N_BIAS_SUB, BIAS_SUB, BIAS_SUB), lambda g, i: (g, 0, 0, 0)),
                  pl.BlockSpec(memory_space=pltpu.SMEM),
                  pl.BlockSpec((1, HEAD_DIM, TQ), lambda g, i: (i, 0, 0)),
                  pl.BlockSpec((hw, TQ), lambda g, i: (g, i))],
        out_specs=pl.BlockSpec((TQ, hw), lambda g, i: (i, g)),
        out_shape=jax.ShapeDtypeStruct((t_len, NSA_HEADS * HEAD_DIM), BF16),
        scratch_shapes=[pltpu.VMEM((NSA_REP, 2 * LANE, TQ), BF16),
                        pltpu.VMEM((NSA_REP, SUBLANE, TQ), F32),
                        pltpu.VMEM((NSA_REP, SUBLANE, TQ), F32),
                        pltpu.VMEM((NSA_REP, HEAD_DIM, TQ), F32),
                        pltpu.VMEM((NSA_REP, TK, TQ), F32),
                        pltpu.VMEM((NSA_REP, TK, TQ), BF16)],
        compiler_params=_cparams(("parallel", "arbitrary")),
        name="nsa_slc_swa",
    )(q_t, mask_t, k_aug, v_t, proj, v_t, bias_win, rel_bias, gate_t, o_cmp_t)


def _xattn_kernel(hb_ref, hf_ref, wq_ref, kv_ref, wo_ref, g_ref, b_ref, of_ref, ob_ref):
    q = _dot(hb_ref[...], wq_ref[...]).astype(BF16)
    scale = XA_HEAD_DIM ** -0.5
    outs = []
    for hh in range(XA_HEADS):
        lo = hh * XA_HEAD_DIM
        k = kv_ref[:, lo:lo + XA_HEAD_DIM]
        v = kv_ref[:, XA_W + lo:XA_W + lo + XA_HEAD_DIM]
        s = _dot_nt(q[:, lo:lo + XA_HEAD_DIM], k) * scale
        e = jnp.exp(s - jnp.max(s, axis=1, keepdims=True))
        p = e / jnp.sum(e, axis=1, keepdims=True)
        outs.append(_dot(p.astype(BF16), v).astype(BF16))
    o = jnp.concatenate(outs, axis=1)
    y = _layer_norm(ALPHA * hf_ref[...] + _dot(o, wo_ref[...]), g_ref[...], b_ref[...])
    of_ref[...] = y
    ob_ref[...] = y.astype(BF16)


def _xattn(hb, hf, wq, kv, wo, g, b, tm):
    m, n = hf.shape
    tm = min(tm, m)
    return pl.pallas_call(
        _xattn_kernel,
        grid=(m // tm,),
        in_specs=[pl.BlockSpec((tm, n), lambda i: (i, 0)),
                  pl.BlockSpec((tm, n), lambda i: (i, 0)),
                  pl.BlockSpec((n, XA_W), lambda i: (0, 0)),
                  pl.BlockSpec((N_MEM, 2 * XA_W), lambda i: (0, 0)),
                  pl.BlockSpec((XA_W, n), lambda i: (0, 0)),
                  pl.BlockSpec((1, n), lambda i: (0, 0)),
                  pl.BlockSpec((1, n), lambda i: (0, 0))],
        out_specs=[pl.BlockSpec((tm, n), lambda i: (i, 0)),
                   pl.BlockSpec((tm, n), lambda i: (i, 0))],
        out_shape=[jax.ShapeDtypeStruct((m, n), F32), jax.ShapeDtypeStruct((m, n), BF16)],
        compiler_params=_cparams(("parallel",)),
        name="xattn",
    )(hb, hf, wq, kv, wo, g.reshape(1, n), b.reshape(1, n))


def _ffn_up_kernel(h_ref, wa_ref, wg_ref, cwa_ref, cwg_ref, cba_ref, cbg_ref, o_ref, ta_ref, tg_ref):
    i = pl.program_id(1)

    @pl.when(i == 0)
    def _():
        ta_ref[...] = jnp.zeros_like(ta_ref)
        tg_ref[...] = jnp.zeros_like(tg_ref)

    hb = h_ref[...]
    tm = hb.shape[0]
    row = lax.broadcasted_iota(jnp.int32, (tm, o_ref.shape[1]), 0)

    def conv(w_ref, cw_ref, cb_ref, tail_ref):
        u = _dot(hb, w_ref[...])
        tail = tail_ref[...]
        u1 = jnp.where(row == 0, tail[SUBLANE - 1:SUBLANE, :], pltpu.roll(u, 1, 0))
        u2 = pltpu.roll(u, 2, 0)
        u2 = jnp.where(row == 0, tail[SUBLANE - 2:SUBLANE - 1, :], u2)
        u2 = jnp.where(row == 1, tail[SUBLANE - 1:SUBLANE, :], u2)
        tail_ref[...] = u[tm - SUBLANE:, :]
        return cw_ref[0:1, :] * u2 + cw_ref[1:2, :] * u1 + cw_ref[2:3, :] * u + cb_ref[...]

    a = conv(wa_ref, cwa_ref, cba_ref, ta_ref)
    gg = conv(wg_ref, cwg_ref, cbg_ref, tg_ref)
    o_ref[...] = (a * _gelu(gg)).astype(o_ref.dtype)


def _ffn_up(hb, w_up, conv_w, conv_b, tm, tn):
    m, k = hb.shape
    tm = min(tm, m)
    nj = D_FF // tn
    cb = conv_b.reshape(1, 2 * D_FF)
    return pl.pallas_call(
        _ffn_up_kernel,
        grid=(nj, m // tm),
        in_specs=[pl.BlockSpec((tm, k), lambda j, i: (i, 0)),
                  pl.BlockSpec((k, tn), lambda j, i: (0, j)),
                  pl.BlockSpec((k, tn), lambda j, i: (0, nj + j)),
                  pl.BlockSpec((3, tn), lambda j, i: (0, j)),
                  pl.BlockSpec((3, tn), lambda j, i: (0, nj + j)),
                  pl.BlockSpec((1, tn), lambda j, i: (0, j)),
                  pl.BlockSpec((1, tn), lambda j, i: (0, nj + j))],
        out_specs=pl.BlockSpec((tm, tn), lambda j, i: (i, j)),
        out_shape=jax.ShapeDtypeStruct((m, D_FF), BF16),
        scratch_shapes=[pltpu.VMEM((SUBLANE, tn), F32), pltpu.VMEM((SUBLANE, tn), F32)],
        compiler_params=_cparams(("parallel", "arbitrary")),
        name="ffn_up",
    )(hb, w_up, w_up, conv_w, conv_w, cb, cb)


def _static_tables(t_len):
    n16 = t_len // CMP_STRIDE
    n_sel = t_len // SEL_BLOCK
    n_cmp = (t_len - CMP_LEN) // CMP_STRIDE + 1
    cmp_start = np.arange(n16) * CMP_STRIDE
    cmp_end = cmp_start + CMP_LEN - 1
    sel = np.arange(n_sel)
    overlap_t = ((cmp_start[None, :] < (sel[:, None] + 1) * SEL_BLOCK)
                 & (cmp_end[None, :] >= sel[:, None] * SEL_BLOCK)
                 & (np.arange(n16)[None, :] < n_cmp))
    overlap_cat = np.concatenate([overlap_t, overlap_t], axis=1)
    tri_t = np.arange(TK)[None, :] > np.arange(TK)[:, None]
    tri_cat = np.concatenate([tri_t, tri_t], axis=1)
    onehot = (np.arange(t_len)[:, None] // SEL_BLOCK) == sel[None, :]
    return (jnp.asarray(overlap_cat, BF16), jnp.asarray(tri_cat, BF16), jnp.asarray(onehot, BF16))


def _split_in_proj(w):
    sizes = (S5_CH, 512, 512, 512, 1024, 256, 256, 256, 256, 256, 256, 3 * NSA_HEADS)
    offs = np.cumsum((0,) + sizes)
    u5, sbq, sbk, sbv, nq, kc, vc, ksl, vsl, ksw, vsw, gt = (
        w[:, int(offs[n]):int(offs[n + 1])] for n in range(len(sizes)))
    w_nat = jnp.concatenate([sbk, kc, vc, ksl, ksw], axis=1).astype(BF16)
    w_q = jnp.concatenate([sbq, nq], axis=1).T.astype(BF16)
    w_v = jnp.concatenate([sbv, vsl, vsw], axis=1).T.astype(BF16)
    return u5.astype(BF16), w_nat, w_q, w_v, _gate_weights_t(gt)


def kernel(x, mem, w_in, w_out, s5_lambda_re, s5_lambda_im, s5_log_dt, s5_b_re, s5_b_im, s5_c_re, s5_c_im, s5_d, s5_w_glu, s5_b_glu, nsa_cmp_pos, nsa_cmp_w1, nsa_cmp_w2, rel_bias, xa_wq, xa_wkv, xa_wo, ffn_w_up, ffn_conv_w, ffn_conv_b, ffn_w_down, ln_g, ln_b):
    t_len = x.shape[1]
    overlap_cat, tri_cat, onehot = _static_tables(t_len)
    bias_win, bias_cmp = _bias_tables(rel_bias)
    mem_b = mem[0].astype(BF16)
    hf = x[0]
    hb = hf.astype(BF16)
    for l in range(DEPTH):
        w_u5, w_nat, w_q, w_v, w_gate_t = _split_in_proj(w_in[l])
        u5 = _mm(hb, w_u5, F32, 512, 512)
        proj = _mm_heads(hb, w_nat, 512)
        q_t = _mm_nt(w_q, hb, BF16, 512, TQ)
        v_t = _mm_nt(w_v, hb, BF16, 512, TK)
        gate_t = _mm_nt(w_gate_t, hb, F32, HEAD_DIM, TQ)

        s5p = _s5_params(s5_lambda_re[l], s5_lambda_im[l], s5_log_dt[l], s5_b_re[l], s5_b_im[l],
                         s5_c_re[l], s5_c_im[l])
        y_s5 = _s5(u5, s5p, s5_w_glu[l], s5_b_glu[l], s5_d[l])
        y_sb = _stick_breaking(proj, q_t, v_t, tri_cat)

        kvc, kvc_t = _compress(proj, HN_KC, nsa_cmp_pos[l], nsa_cmp_w1[l], nsa_cmp_w2[l])
        o_cmp_t, mask_t = _cmp_select(q_t, kvc, kvc_t, bias_cmp, rel_bias, overlap_cat)
        y_nsa = _slc_swa(proj, q_t, v_t, mask_t, _key_augment(proj, onehot), bias_win, rel_bias,
                         gate_t, o_cmp_t)

        y = jnp.concatenate([y_s5, y_sb, y_nsa], axis=1)
        hf, hb = _mm_ln(y, w_out[l].astype(BF16), hf, ln_g[l, 0], ln_b[l, 0], 512, 512)

        kv = _mm(mem_b, xa_wkv[l].astype(BF16), BF16, N_MEM, 512)
        hf, hb = _xattn(hb, hf, xa_wq[l].astype(BF16), kv, xa_wo[l].astype(BF16),
                        ln_g[l, 1], ln_b[l, 1], 256)

        act = _ffn_up(hb, ffn_w_up[l].astype(BF16), ffn_conv_w[l], ffn_conv_b[l], 512, 512)
        hf, hb = _mm_ln(act, ffn_w_down[l].astype(BF16), hf, ln_g[l, 2], ln_b[l, 2], 512, 512)
    return hf[None]
```

```python
import functools
import math

import numpy as np
import jax
import jax.numpy as jnp
from jax import lax
from jax.experimental import pallas as pl
from jax.experimental.pallas import tpu as pltpu

F32 = jnp.float32
BF16 = jnp.bfloat16

D_MODEL = 2048
DEPTH = 4
HEAD_DIM = 64
S5_CH = 512
S5_GROUP = 16
S5_NG = 32
S5_P = 64
SB_HEADS = 8
NSA_HEADS = 16
NSA_KV = 4
NSA_REP = 4
CMP_LEN = 32
CMP_STRIDE = 16
CMP_HID = 128
SEL_BLOCK = 64
SEL_TOPN = 16
WINDOW = 512
FORCE_SCORE = 1e4
N_BUCKETS = 32
MAX_DIST = 1024
N_MEM = 256
XA_HEADS = 4
XA_HEAD_DIM = 128
XA_W = 512
D_FF = 5632
ALPHA = (2.0 * DEPTH) ** 0.25
LN_EPS = 1e-5
NEG = -1e30

LANE = 128
SUBLANE = 8
MXU_DIM = 256
VMEM_LIMIT = 56 * 1024 * 1024
TQ = 512
TK = 256
SUB_Q = TQ // LANE
SUB_K = TK // LANE
HEADS_PER_STEP = 4
EXP_ROWS = 32
S5_SEG = 64
S5_CHUNK = SUBLANE * S5_SEG
S5_CB = 8
BIAS_SUB = LANE
FAR_SUB = MAX_DIST // BIAS_SUB + 1
N_BIAS_SUB = FAR_SUB + SUB_K + SUB_Q - 2

HT_SBQ = 0
HT_NQ = HT_SBQ + SB_HEADS
HT_Q_ROWS = (HT_NQ + NSA_HEADS) * HEAD_DIM
HT_SBV = 0
HT_VSL = HT_SBV + SB_HEADS
HT_VSW = HT_VSL + NSA_KV
HT_V_ROWS = (HT_VSW + NSA_KV) * HEAD_DIM
HN_SBK = 0
HN_KC = HN_SBK + SB_HEADS
HN_KSL = HN_KC + 2 * NSA_KV
HN_KSW = HN_KSL + NSA_KV
HN_HEADS = HN_KSW + NSA_KV


def _cparams(sem):
    return pltpu.CompilerParams(dimension_semantics=sem, vmem_limit_bytes=VMEM_LIMIT)


def _dot(a, b):
    return jnp.dot(a, b, preferred_element_type=F32)


def _dot_nt(a, b):
    return lax.dot_general(a, b, (((1,), (1,)), ((), ())), preferred_element_type=F32)


def _split_bf16(x):
    hi = x.astype(BF16)
    lo = (x - hi.astype(F32)).astype(BF16)
    return hi, lo


def _gelu(x):
    c = math.sqrt(2.0 / math.pi)
    return 0.5 * x * (1.0 + jnp.tanh(c * (x + 0.044715 * (x * x * x))))


def _sigmoid(x):
    return 1.0 / (1.0 + jnp.exp(-x))


def _layer_norm(x, g, b):
    mu = jnp.mean(x, axis=-1, keepdims=True)
    xc = x - mu
    var = jnp.mean(xc * xc, axis=-1, keepdims=True)
    return xc * lax.rsqrt(var + LN_EPS) * g + b


def _mm_kernel(a_ref, b_ref, o_ref):
    o_ref[...] = _dot(a_ref[...], b_ref[...]).astype(o_ref.dtype)


def _mm(a, b, out_dtype, tm, tn):
    m, k = a.shape
    n = b.shape[1]
    tm = min(tm, m)
    return pl.pallas_call(
        _mm_kernel,
        grid=(n // tn, m // tm),
        in_specs=[pl.BlockSpec((tm, k), lambda j, i: (i, 0)),
                  pl.BlockSpec((k, tn), lambda j, i: (0, j))],
        out_specs=pl.BlockSpec((tm, tn), lambda j, i: (i, j)),
        out_shape=jax.ShapeDtypeStruct((m, n), out_dtype),
        compiler_params=_cparams(("parallel", "arbitrary")),
        name="mm",
    )(a, b)


def _mm_heads_kernel(a_ref, b_ref, o_ref):
    o = _dot(a_ref[...], b_ref[...]).astype(o_ref.dtype)
    for r in range(o_ref.shape[0]):
        o_ref[r] = o[:, r * HEAD_DIM:(r + 1) * HEAD_DIM]


def _mm_heads(a, b, tm):
    m, k = a.shape
    n = b.shape[1]
    tn = MXU_DIM
    hpt = tn // HEAD_DIM
    tm = min(tm, m)
    return pl.pallas_call(
        _mm_heads_kernel,
        grid=(n // tn, m // tm),
        in_specs=[pl.BlockSpec((tm, k), lambda j, i: (i, 0)),
                  pl.BlockSpec((k, tn), lambda j, i: (0, j))],
        out_specs=pl.BlockSpec((hpt, tm, HEAD_DIM), lambda j, i: (j, i, 0)),
        out_shape=jax.ShapeDtypeStruct((n // HEAD_DIM, m, HEAD_DIM), BF16),
        compiler_params=_cparams(("parallel", "arbitrary")),
        name="mm_heads",
    )(a, b)


def _mm_nt_kernel(w_ref, a_ref, o_ref):
    o_ref[0] = _dot_nt(w_ref[...], a_ref[...]).astype(o_ref.dtype)


def _mm_nt(w_t, a, out_dtype, tn, tm):
    n, k = w_t.shape
    m = a.shape[0]
    return pl.pallas_call(
        _mm_nt_kernel,
        grid=(n // tn, m // tm),
        in_specs=[pl.BlockSpec((tn, k), lambda j, i: (j, 0)),
                  pl.BlockSpec((tm, k), lambda j, i: (i, 0))],
        out_specs=pl.BlockSpec((1, tn, tm), lambda j, i: (i, j, 0)),
        out_shape=jax.ShapeDtypeStruct((m // tm, n, tm), out_dtype),
        compiler_params=_cparams(("parallel", "arbitrary")),
        name="mm_nt",
    )(w_t, a)


def _mm_ln_kernel(a_ref, w_ref, h_ref, g_ref, b_ref, of_ref, ob_ref, acc_ref):
    kk = pl.program_id(1)

    @pl.when(kk == 0)
    def _():
        acc_ref[...] = jnp.zeros_like(acc_ref)

    acc_ref[...] += _dot(a_ref[...], w_ref[...])

    @pl.when(kk == pl.num_programs(1) - 1)
    def _():
        y = _layer_norm(ALPHA * h_ref[...] + acc_ref[...], g_ref[...], b_ref[...])
        of_ref[...] = y
        ob_ref[...] = y.astype(BF16)


def _mm_ln(a, w, h, g, b, tm, tk):
    m, k = a.shape
    n = w.shape[1]
    tm = min(tm, m)
    return pl.pallas_call(
        _mm_ln_kernel,
        grid=(m // tm, k // tk),
        in_specs=[pl.BlockSpec((tm, tk), lambda i, kk: (i, kk)),
                  pl.BlockSpec((tk, n), lambda i, kk: (kk, 0)),
                  pl.BlockSpec((tm, n), lambda i, kk: (i, 0)),
                  pl.BlockSpec((1, n), lambda i, kk: (0, 0)),
                  pl.BlockSpec((1, n), lambda i, kk: (0, 0))],
        out_specs=[pl.BlockSpec((tm, n), lambda i, kk: (i, 0)),
                   pl.BlockSpec((tm, n), lambda i, kk: (i, 0))],
        out_shape=[jax.ShapeDtypeStruct((m, n), F32), jax.ShapeDtypeStruct((m, n), BF16)],
        scratch_shapes=[pltpu.VMEM((tm, n), F32)],
        compiler_params=_cparams(("parallel", "arbitrary")),
        name="mm_ln",
    )(a, w, h, g.reshape(1, n), b.reshape(1, n))


def _s5_kernel(u_ref, bm_ref, cm_ref, a_ref, d_ref, z_ref, s_ref, p_ref, carry_ref):
    tc = pl.program_id(1)
    w = s_ref.shape[1] // 2
    n_step = p_ref.shape[0]
    ar1 = a_ref[0:1, :]
    ai1 = a_ref[1:2, :]

    @pl.when(tc == 0)
    def _():
        carry_ref[...] = jnp.zeros_like(carry_ref)
        p_ref[0:1, 0:w] = ar1
        p_ref[0:1, w:2 * w] = ai1

        def pw(t, c):
            pr, pi = c
            nr = pr * ar1 - pi * ai1
            ni = pr * ai1 + pi * ar1
            p_ref[pl.ds(t, 1), 0:w] = nr
            p_ref[pl.ds(t, 1), w:2 * w] = ni
            return nr, ni

        lax.fori_loop(1, n_step, pw, (ar1, ai1))

    u = u_ref[...]
    s_ref[...] = _dot(u.astype(BF16), bm_ref[0])

    ar = jnp.broadcast_to(ar1, (SUBLANE, w))
    ai = jnp.broadcast_to(ai1, (SUBLANE, w))

    def step(t, c):
        sr, si = c
        r0 = pl.multiple_of(t * SUBLANE, SUBLANE)
        xr = s_ref[pl.ds(r0, SUBLANE), 0:w]
        xi = s_ref[pl.ds(r0, SUBLANE), w:2 * w]
        nr = ar * sr - ai * si + xr
        ni = ar * si + ai * sr + xi
        s_ref[pl.ds(r0, SUBLANE), 0:w] = nr
        s_ref[pl.ds(r0, SUBLANE), w:2 * w] = ni
        return nr, ni

    zero = jnp.zeros((SUBLANE, w), F32)
    fr, fi = lax.fori_loop(0, n_step, step, (zero, zero), unroll=8)

    amr = p_ref[n_step - 1:n_step, 0:w]
    ami = p_ref[n_step - 1:n_step, w:2 * w]
    row = lax.broadcasted_iota(jnp.int32, (SUBLANE, w), 0)
    cr = carry_ref[0:1, 0:w]
    ci = carry_ref[0:1, w:2 * w]
    cin_r = jnp.zeros((SUBLANE, w), F32)
    cin_i = jnp.zeros((SUBLANE, w), F32)
    for seg in range(SUBLANE):
        cin_r = jnp.where(row == seg, cr, cin_r)
        cin_i = jnp.where(row == seg, ci, cin_i)
        nr = fr[seg:seg + 1, :] + amr * cr - ami * ci
        ni = fi[seg:seg + 1, :] + amr * ci + ami * cr
        cr, ci = nr, ni
    carry_ref[0:1, 0:w] = cr
    carry_ref[0:1, w:2 * w] = ci

    def fix(t, c):
        r0 = pl.multiple_of(t * SUBLANE, SUBLANE)
        pr = p_ref[pl.ds(t, 1), 0:w]
        pi = p_ref[pl.ds(t, 1), w:2 * w]
        s_ref[pl.ds(r0, SUBLANE), 0:w] += pr * cin_r - pi * cin_i
        s_ref[pl.ds(r0, SUBLANE), w:2 * w] += pr * cin_i + pi * cin_r
        return c

    lax.fori_loop(0, n_step, fix, 0, unroll=8)

    y = _dot(s_ref[...].astype(BF16), cm_ref[0]) + d_ref[...] * u
    z_ref[...] = _gelu(y)


def _s5_glu_kernel(z_ref, w_ref, b_ref, o_ref):
    z = z_ref[...]
    gate = _sigmoid(_dot(z.astype(BF16), w_ref[...]) + b_ref[...])
    o_ref[...] = (z * gate).astype(o_ref.dtype)


def _s5_params(lam_re, lam_im, log_dt, b_re, b_im, c_re, c_im):
    delta = jnp.exp(log_dt)[:, None]
    mag = jnp.exp(lam_re * delta)
    ar = mag * jnp.cos(lam_im * delta)
    ai = mag * jnp.sin(lam_im * delta)
    den = lam_re * lam_re + lam_im * lam_im
    cr = ((ar - 1.0) * lam_re + ai * lam_im) / den
    ci = (ai * lam_re - (ar - 1.0) * lam_im) / den
    br = cr[..., None] * b_re - ci[..., None] * b_im
    bi = cr[..., None] * b_im + ci[..., None] * b_re
    ncb = S5_NG // S5_CB
    eye = jnp.eye(S5_CB, dtype=F32)

    def in_mat(b):
        b = b.reshape(ncb, S5_CB, S5_P, S5_GROUP)
        m = jnp.einsum('ngpc,gh->ngchp', b, eye)
        return m.reshape(ncb, S5_CB * S5_GROUP, S5_CB * S5_P)

    def out_mat(c):
        c = c.reshape(ncb, S5_CB, S5_GROUP, S5_P)
        m = jnp.einsum('ngcp,gh->ngphc', c, eye)
        return m.reshape(ncb, S5_CB * S5_P, S5_CB * S5_GROUP)

    bm = jnp.concatenate([in_mat(br), in_mat(bi)], axis=2).astype(BF16)
    cm = jnp.concatenate([out_mat(c_re), -out_mat(c_im)], axis=1).astype(BF16)
    a = jnp.stack([ar.reshape(-1), ai.reshape(-1)], axis=0)
    return bm, cm, a


def _s5(u, p, w_glu, b_glu, d):
    t_len = u.shape[0]
    nc = t_len // S5_CHUNK
    ncb = S5_NG // S5_CB
    wl = S5_CB * S5_GROUP
    ws = S5_CB * S5_P
    bm, cm, a = p
    up = u.reshape(nc, SUBLANE, S5_SEG, S5_CH).transpose(0, 2, 1, 3).reshape(t_len, S5_CH)
    z = pl.pallas_call(
        _s5_kernel,
        grid=(ncb, nc),
        in_specs=[pl.BlockSpec((S5_CHUNK, wl), lambda c, t: (t, c)),
                  pl.BlockSpec((1, wl, 2 * ws), lambda c, t: (c, 0, 0)),
                  pl.BlockSpec((1, 2 * ws, wl), lambda c, t: (c, 0, 0)),
                  pl.BlockSpec((2, ws), lambda c, t: (0, c)),
                  pl.BlockSpec((1, wl), lambda c, t: (0, c))],
        out_specs=pl.BlockSpec((S5_CHUNK, wl), lambda c, t: (t, c)),
        out_shape=jax.ShapeDtypeStruct((t_len, S5_CH), F32),
        scratch_shapes=[pltpu.VMEM((S5_CHUNK, 2 * ws), F32),
                        pltpu.VMEM((S5_SEG, 2 * ws), F32),
                        pltpu.VMEM((SUBLANE, 2 * ws), F32)],
        compiler_params=_cparams(("parallel", "arbitrary")),
        name="s5_scan",
    )(up, bm, cm, a, d.reshape(1, S5_CH))
    tm = min(1024, t_len)
    y = pl.pallas_call(
        _s5_glu_kernel,
        grid=(t_len // tm,),
        in_specs=[pl.BlockSpec((tm, S5_CH), lambda i: (i, 0)),
                  pl.BlockSpec((S5_CH, S5_CH), lambda i: (0, 0)),
                  pl.BlockSpec((1, S5_CH), lambda i: (0, 0))],
        out_specs=pl.BlockSpec((tm, S5_CH), lambda i: (i, 0)),
        out_shape=jax.ShapeDtypeStruct((t_len, S5_CH), BF16),
        compiler_params=_cparams(("parallel",)),
        name="s5_glu",
    )(z, w_glu.astype(BF16), b_glu.reshape(1, S5_CH))
    return y.reshape(nc, S5_SEG, SUBLANE, S5_CH).transpose(0, 2, 1, 3).reshape(t_len, S5_CH)


def _sb_kernel(q_ref, k_ref, v_ref, u_ref, o_ref, r_ref, acc_ref, nl_ref, arg_ref, w_ref):
    qi = pl.program_id(1)
    ucat = u_ref[...]
    qs = [q_ref[0, r * HEAD_DIM:(r + 1) * HEAD_DIM, :] * 0.125 for r in range(HEADS_PER_STEP)]
    r_ref[...] = jnp.zeros_like(r_ref)
    acc_ref[...] = jnp.zeros_like(acc_ref)
    key = lax.broadcasted_iota(jnp.int32, (TK, TQ), 0)
    qry = lax.broadcasted_iota(jnp.int32, (TK, TQ), 1)

    def tile(kt, diag_off):
        k0 = pl.multiple_of(kt * TK, TK)
        mask = None if diag_off is None else (key + diag_off) < qry
        sums = []
        for r in range(HEADS_PER_STEP):
            z = _dot(k_ref[r, pl.ds(k0, TK), :], qs[r])
            nl = jnp.maximum(z, 0.0) + jnp.log(1.0 + jnp.exp(-jnp.abs(z)))
            arg_ref[r] = z - nl
            if mask is not None:
                nl = jnp.where(mask, nl, 0.0)
            hi, lo = _split_bf16(nl)
            nl_ref[r, 0:TK, :] = hi
            nl_ref[r, TK:2 * TK, :] = lo
            sums.append(jnp.sum(nl, axis=0, keepdims=True))
        for r in range(HEADS_PER_STEP):
            later = _dot(ucat, nl_ref[r])
            w = jnp.exp(arg_ref[r] - later)
            if mask is not None:
                w = jnp.where(mask, w, 0.0)
            w_ref[r] = w.astype(BF16)
        for r in range(HEADS_PER_STEP):
            v = v_ref[kt, r * HEAD_DIM:(r + 1) * HEAD_DIM, :]
            r_old = r_ref[r, 0:1, :]
            acc_ref[r] += _dot(v, w_ref[r]) * jnp.exp(-r_old)
            r_ref[r, 0:1, :] = r_old + sums[r]

    n_diag = TQ // TK
    for c in reversed(range(n_diag)):
        tile(n_diag * qi + c, c * TK)

    def alive():
        return (jnp.max(jnp.exp(-r_ref[:, 0:1, :])) > 0.0).astype(jnp.int32)

    def cond(state):
        return (state[0] >= 0) & (state[1] > 0)

    def body(state):
        tile(state[0], None)
        return state[0] - 1, alive()

    lax.while_loop(cond, body, (n_diag * qi - 1, alive()))
    o_t = jnp.concatenate([acc_ref[r] for r in range(HEADS_PER_STEP)], axis=0)
    o_ref[...] = o_t.T.astype(o_ref.dtype)


def _stick_breaking(proj, q_t, v_t, ucat):
    t_len = proj.shape[1]
    nt = t_len // TQ
    hw = HEADS_PER_STEP * HEAD_DIM
    return pl.pallas_call(
        _sb_kernel,
        grid=(SB_HEADS // HEADS_PER_STEP, nt),
        in_specs=[pl.BlockSpec((1, hw, TQ), lambda h, i: (i, HT_SBQ // HEADS_PER_STEP + h, 0)),
                  pl.BlockSpec((HEADS_PER_STEP, t_len, HEAD_DIM),
                               lambda h, i: (HN_SBK // HEADS_PER_STEP + h, 0, 0)),
                  pl.BlockSpec((t_len // TK, hw, TK),
                               lambda h, i: (0, HT_SBV // HEADS_PER_STEP + h, 0)),
                  pl.BlockSpec((TK, 2 * TK), lambda h, i: (0, 0))],
        out_specs=pl.BlockSpec((TQ, hw), lambda h, i: (i, h)),
        out_shape=jax.ShapeDtypeStruct((t_len, SB_HEADS * HEAD_DIM), BF16),
        scratch_shapes=[pltpu.VMEM((HEADS_PER_STEP, SUBLANE, TQ), F32),
                        pltpu.VMEM((HEADS_PER_STEP, HEAD_DIM, TQ), F32),
                        pltpu.VMEM((HEADS_PER_STEP, 2 * TK, TQ), BF16),
                        pltpu.VMEM((HEADS_PER_STEP, TK, TQ), F32),
                        pltpu.VMEM((HEADS_PER_STEP, TK, TQ), BF16)],
        compiler_params=_cparams(("parallel", "arbitrary")),
        name="stick_breaking",
    )(q_t, proj, v_t, ucat)


def _t5_bucket(dist):
    n = jnp.maximum(dist, 0)
    max_exact = N_BUCKETS // 2
    nf = jnp.maximum(n, 1).astype(jnp.float32)
    large = max_exact + (jnp.log(nf / max_exact) / math.log(MAX_DIST / max_exact)
                         * (N_BUCKETS - max_exact)).astype(jnp.int32)
    large = jnp.minimum(large, N_BUCKETS - 1)
    return jnp.where(n < max_exact, n, large)


def _bias_lookup_kernel(tab_ref, idx_ref, o_ref):
    h = pl.program_id(0)
    idx = idx_ref[...]
    acc = jnp.zeros(idx.shape, F32)
    for b in range(N_BUCKETS):
        acc = jnp.where(idx == b, tab_ref[b, h], acc)
    o_ref[0] = acc


def _bias_lookup(tab, idx):
    r, c = idx.shape
    n_h = tab.shape[1]
    return pl.pallas_call(
        _bias_lookup_kernel,
        grid=(n_h,),
        in_specs=[pl.BlockSpec(memory_space=pltpu.SMEM),
                  pl.BlockSpec((r, c), lambda h: (0, 0))],
        out_specs=pl.BlockSpec((1, r, c), lambda h: (h, 0, 0)),
        out_shape=jax.ShapeDtypeStruct((n_h, r, c), F32),
        compiler_params=_cparams(("arbitrary",)),
        name="bias_lookup",
    )(tab, idx)


def _bias_tables(rel_bias):
    i = jnp.arange(BIAS_SUB)
    d_win = (BIAS_SUB * jnp.arange(N_BIAS_SUB)[:, None, None] + i[None, None, :] - i[None, :, None])
    idx_win = _t5_bucket(d_win).reshape(N_BIAS_SUB * BIAS_SUB, BIAS_SUB)
    win = _bias_lookup(rel_bias, idx_win).reshape(NSA_HEADS, N_BIAS_SUB, BIAS_SUB, BIAS_SUB)
    n_m = (LANE * CMP_STRIDE) // TQ
    m = jnp.arange(n_m)[:, None, None]
    c = jnp.arange(2 * LANE)[None, :, None]
    qi = jnp.arange(TQ)[None, None, :]
    d_cmp = TQ * m + qi + LANE * CMP_STRIDE - CMP_STRIDE * c - (CMP_LEN - 1)
    idx_cmp = _t5_bucket(d_cmp).reshape(n_m * 2 * LANE, TQ)
    cmp_t = _bias_lookup(rel_bias, idx_cmp).reshape(NSA_HEADS, n_m, 2 * LANE, TQ)
    return win, cmp_t


def _compress_kernel(c0_ref, c1_ref, w1_ref, pos_ref, w2_ref, w2t_ref, o_ref, ot_ref):
    half = c0_ref.shape[3]
    w1 = w1_ref[0]
    posb = _dot(pos_ref[0], w1)[0:1, :]
    hid = _dot(c0_ref[0, 0], w1[0:half, :]) + _dot(c1_ref[0, 0], w1[half:, :]) + posb
    act = _gelu(hid).astype(BF16)
    o_ref[0, 0] = _dot(act, w2_ref[0]).astype(o_ref.dtype)
    ot_ref[0, 0] = _dot_nt(w2t_ref[0], act).astype(ot_ref.dtype)


def _compress(proj, head0, cmp_pos, cmp_w1, cmp_w2):
    t_len = proj.shape[1]
    n16 = t_len // CMP_STRIDE
    half = CMP_STRIDE * HEAD_DIM
    kv = proj[head0:head0 + 2 * NSA_KV].reshape(2, NSA_KV, n16, half)
    kv_next = jnp.concatenate([kv[:, :, 1:], jnp.zeros((2, NSA_KV, 1, half), BF16)], axis=2)
    pos = jnp.broadcast_to(cmp_pos.reshape(2, 1, CMP_LEN * HEAD_DIM), (2, SUBLANE, CMP_LEN * HEAD_DIM))
    w2 = cmp_w2.astype(BF16)
    return pl.pallas_call(
        _compress_kernel,
        grid=(2, NSA_KV),
        in_specs=[pl.BlockSpec((1, 1, n16, half), lambda j, g: (j, g, 0, 0)),
                  pl.BlockSpec((1, 1, n16, half), lambda j, g: (j, g, 0, 0)),
                  pl.BlockSpec((1, 2 * half, CMP_HID), lambda j, g: (j, 0, 0)),
                  pl.BlockSpec((1, SUBLANE, 2 * half), lambda j, g: (j, 0, 0)),
                  pl.BlockSpec((1, CMP_HID, HEAD_DIM), lambda j, g: (j, 0, 0)),
                  pl.BlockSpec((1, HEAD_DIM, CMP_HID), lambda j, g: (j, 0, 0))],
        out_specs=[pl.BlockSpec((1, 1, n16, HEAD_DIM), lambda j, g: (j, g, 0, 0)),
                   pl.BlockSpec((1, 1, HEAD_DIM, n16), lambda j, g: (j, g, 0, 0))],
        out_shape=[jax.ShapeDtypeStruct((2, NSA_KV, n16, HEAD_DIM), BF16),
                   jax.ShapeDtypeStruct((2, NSA_KV, HEAD_DIM, n16), BF16)],
        compiler_params=_cparams(("parallel", "arbitrary")),
        name="nsa_compress",
    )(kv, kv_next, cmp_w1.astype(BF16), pos.astype(BF16), w2, w2.transpose(0, 2, 1))


def _cmp_select_kernel(q_ref, kc_ref, vct_ref, bias_ref, tab_ref, ov_ref, oc_ref, mn_ref, *, n_cmp):
    g = pl.program_id(0)
    qi = pl.program_id(1)
    n16 = kc_ref.shape[2]
    n_ct = n16 // LANE
    kc = kc_ref[0, 0]
    vct = vct_ref[0, 0]
    q_per_ct = (LANE * CMP_STRIDE) // TQ
    jd = qi // q_per_ct
    n = lax.broadcasted_iota(jnp.int32, (n16, TQ), 0)
    t = qi * TQ + lax.broadcasted_iota(jnp.int32, (n16, TQ), 1)
    valid = (CMP_STRIDE * n + (CMP_LEN - 1) <= t) & (n < n_cmp)
    psum = jnp.zeros((n16, TQ), F32)
    for r in range(NSA_REP):
        q = q_ref[0, r * HEAD_DIM:(r + 1) * HEAD_DIM, :] * 0.125
        far = tab_ref[N_BUCKETS - 1, g * NSA_REP + r]
        rows = []
        for jc in range(n_ct):
            near = jnp.where(jc == jd, bias_ref[r, 0, LANE:2 * LANE, :], bias_ref[r, 0, 0:LANE, :])
            rows.append(jnp.where((jc == jd) | (jc == jd - 1), near, far))
        bias = rows[0] if n_ct == 1 else jnp.concatenate(rows, axis=0)
        s = jnp.where(valid, _dot(kc, q) + bias, NEG)
        mx = jnp.max(s, axis=0, keepdims=True)
        e = jnp.where(valid, jnp.exp(s - mx), 0.0)
        p = e * (1.0 / jnp.maximum(jnp.sum(e, axis=0, keepdims=True), 1e-30))
        psum = psum + p
        oc_ref[r * HEAD_DIM:(r + 1) * HEAD_DIM, :] = _dot(vct, p.astype(BF16))
    hi, lo = _split_bf16(psum)
    imp = _dot(ov_ref[...], jnp.concatenate([hi, lo], axis=0))
    n_sel = imp.shape[0]
    j = lax.broadcasted_iota(jnp.int32, (n_sel, LANE), 0)
    jf = j.astype(F32)
    for cc in range(TQ // LANE):
        tq = qi * TQ + cc * LANE + lax.broadcasted_iota(jnp.int32, (n_sel, LANE), 1)
        cur = tq // SEL_BLOCK
        forced = (j == 0) | (j == cur) | (j == cur - 1)
        score = jnp.where(j <= cur,
                          jnp.where(forced, FORCE_SCORE, imp[:, cc * LANE:(cc + 1) * LANE]), -1.0)
        sel = jnp.zeros((n_sel, LANE), F32)
        for _ in range(min(SEL_TOPN, n_sel)):
            mx = jnp.max(score, axis=0, keepdims=True)
            first = jnp.min(jnp.where(score == mx, jf, float(n_sel)), axis=0, keepdims=True)
            hit = jf == first
            sel = jnp.where(hit, 1.0, sel)
            score = jnp.where(hit, -jnp.inf, score)
        mn_ref[0, :, cc * LANE:(cc + 1) * LANE] = (sel - 1.0).astype(mn_ref.dtype)


def _cmp_select(q_t, kc, vct, bias_cmp, rel_bias, overlap_cat):
    nt = q_t.shape[0]
    t_len = nt * TQ
    n16 = kc.shape[2]
    n_sel = t_len // SEL_BLOCK
    n_cmp = (t_len - CMP_LEN) // CMP_STRIDE + 1
    q_per_ct = (LANE * CMP_STRIDE) // TQ
    hw = NSA_REP * HEAD_DIM
    return pl.pallas_call(
        functools.partial(_cmp_select_kernel, n_cmp=n_cmp),
        grid=(NSA_KV, nt),
        in_specs=[pl.BlockSpec((1, hw, TQ), lambda g, i: (i, HT_NQ // NSA_REP + g, 0)),
                  pl.BlockSpec((1, 1, n16, HEAD_DIM), lambda g, i: (0, g, 0, 0)),
                  pl.BlockSpec((1, 1, HEAD_DIM, n16), lambda g, i: (1, g, 0, 0)),
                  pl.BlockSpec((NSA_REP, 1, 2 * LANE, TQ), lambda g, i: (g, i % q_per_ct, 0, 0)),
                  pl.BlockSpec(memory_space=pltpu.SMEM),
                  pl.BlockSpec((n_sel, 2 * n16), lambda g, i: (0, 0))],
        out_specs=[pl.BlockSpec((hw, TQ), lambda g, i: (g, i)),
                   pl.BlockSpec((1, n_sel, TQ), lambda g, i: (g, 0, i))],
        out_shape=[jax.ShapeDtypeStruct((NSA_HEADS * HEAD_DIM, t_len), F32),
                   jax.ShapeDtypeStruct((NSA_KV, n_sel, t_len), BF16)],
        compiler_params=_cparams(("parallel", "arbitrary")),
        name="nsa_cmp_select",
    )(q_t, kc, vct, bias_cmp, rel_bias, overlap_cat)


def _bias_tile(win_ref, r, base):
    rows = []
    for b in range(SUB_K):
        cols = []
        for a in range(SUB_Q):
            idx = jnp.maximum(base + a - b, 0)
            cols.append(win_ref[r, idx])
        rows.append(jnp.concatenate(cols, axis=1))
    return jnp.concatenate(rows, axis=0)


def _slc_swa_kernel(q_ref, mn_ref, ka_ref, vs_ref, kw_ref, vw_ref, win_ref, tab_ref, gate_ref,
                    oc_ref, o_ref, qa_ref, m_ref, l_ref, acc_ref, s_ref, p_ref):
    g = pl.program_id(0)
    qi = pl.program_id(1)
    n_sel = mn_ref.shape[1]
    n_diag = TQ // TK
    big = mn_ref[0] * jnp.asarray(1e30, BF16)
    for r in range(NSA_REP):
        qa_ref[r, 0:HEAD_DIM, :] = q_ref[0, r * HEAD_DIM:(r + 1) * HEAD_DIM, :] * 0.125
        qa_ref[r, HEAD_DIM:LANE, :] = jnp.zeros((LANE - HEAD_DIM, TQ), BF16)
        qa_ref[r, LANE:LANE + n_sel, :] = big
        if n_sel < LANE:
            qa_ref[r, LANE + n_sel:2 * LANE, :] = jnp.zeros((LANE - n_sel, TQ), BF16)
    m_ref[...] = jnp.full_like(m_ref, NEG)
    l_ref[...] = jnp.zeros_like(l_ref)
    acc_ref[...] = jnp.zeros_like(acc_ref)
    key = lax.broadcasted_iota(jnp.int32, (TK, TQ), 0)
    qry = lax.broadcasted_iota(jnp.int32, (TK, TQ), 1)

    def scores(kt, far, diag_off, buf):
        k0 = pl.multiple_of(kt * TK, TK)
        ka = ka_ref[0, pl.ds(k0, TK), :]
        for r in range(NSA_REP):
            s = _dot(ka, qa_ref[r])
            if not far:
                s = s + _bias_tile(win_ref, r, SUB_Q * qi - SUB_K * kt)
            if diag_off is not None:
                s = jnp.where(key + diag_off <= qry, s, NEG)
            s_ref[buf, r] = s

    def consume(kt, buf):
        v = vs_ref[kt]
        stats = []
        groups = EXP_ROWS // SUBLANE

        def rows8(c0):
            return s_ref[buf, r, c0:c0 + EXP_ROWS, :].reshape(groups, SUBLANE, TQ)

        for r in range(NSA_REP):
            mx8 = jnp.max(rows8(0), axis=0)
            for c0 in range(EXP_ROWS, TK, EXP_ROWS):
                mx8 = jnp.maximum(mx8, jnp.max(rows8(c0), axis=0))
            m_old = m_ref[r, 0:1, :]
            m_new = jnp.maximum(m_old, jnp.max(mx8, axis=0, keepdims=True))
            m_ref[r, 0:1, :] = m_new
            stats.append((m_new, jnp.exp(m_old - m_new)))
        for r in range(NSA_REP):
            m_new, alpha = stats[r]
            ps8 = jnp.zeros((SUBLANE, TQ), F32)
            for c0 in range(0, TK, EXP_ROWS):
                p = jnp.exp(s_ref[buf, r, c0:c0 + EXP_ROWS, :] - m_new)
                ps8 = ps8 + jnp.sum(p.reshape(groups, SUBLANE, TQ), axis=0)
                p_ref[r, c0:c0 + EXP_ROWS, :] = p.astype(BF16)
            l_ref[r, 0:1, :] = alpha * l_ref[r, 0:1, :] + jnp.sum(ps8, axis=0, keepdims=True)
        for r in range(NSA_REP):
            acc_ref[r] = stats[r][1] * acc_ref[r] + _dot(v, p_ref[r])

    def slc_tile(kt, far, diag_off):
        scores(kt, far, diag_off, 0)
        consume(kt, 0)

    n_far = jnp.maximum((SUB_Q * qi - (SUB_K - 1) - FAR_SUB) // SUB_K + 1, 0)

    scores(0, True, None, 0)

    def far_body(j, c):
        scores(2 * j + 1, True, None, 1)
        consume(2 * j, 0)
        scores(2 * j + 2, True, None, 0)
        consume(2 * j + 1, 1)
        return c

    lax.fori_loop(0, n_far // 2, far_body, 0)

    @pl.when(n_far % 2 == 1)
    def _():
        consume(n_far - 1, 0)
    for r in range(NSA_REP):
        m_ref[r, 0:1, :] = m_ref[r, 0:1, :] + tab_ref[N_BUCKETS - 1, g * NSA_REP + r]

    def band_body(kt, c):
        slc_tile(kt, False, None)
        return c

    lax.fori_loop(n_far, n_diag * qi, band_body, 0)
    for c in range(n_diag):
        slc_tile(n_diag * qi + c, False, c * TK)

    n_wt = WINDOW // TK + n_diag
    outs = []
    for r in range(NSA_REP):
        q = qa_ref[r, 0:HEAD_DIM, :]
        ss, vv, oks = [], [], []
        for c in range(n_wt):
            back = WINDOW // TK - c
            kt = n_diag * qi - back
            ktc = jnp.maximum(kt, 0)
            k0 = pl.multiple_of(ktc * TK, TK)
            s = _dot(kw_ref[0, pl.ds(k0, TK), :], q) + _bias_tile(win_ref, r, SUB_K * back)
            dist = back * TK + qry - key
            ok = (dist >= 0) & (dist < WINDOW) & (kt >= 0)
            ss.append(jnp.where(ok, s, NEG))
            oks.append(ok)
            vv.append(vw_ref[ktc])
        mx = jnp.max(ss[0], axis=0, keepdims=True)
        for s in ss[1:]:
            mx = jnp.maximum(mx, jnp.max(s, axis=0, keepdims=True))
        den = jnp.zeros((1, TQ), F32)
        o_w = jnp.zeros((HEAD_DIM, TQ), F32)
        for s, v, ok in zip(ss, vv, oks):
            e = jnp.where(ok, jnp.exp(s - mx), 0.0)
            den = den + jnp.sum(e, axis=0, keepdims=True)
            o_w = o_w + _dot(v, e.astype(BF16))
        o_w = o_w * (1.0 / jnp.maximum(den, 1e-30))
        o_s = acc_ref[r] * (1.0 / jnp.maximum(l_ref[r, 0:1, :], 1e-30))
        o_c = oc_ref[r * HEAD_DIM:(r + 1) * HEAD_DIM, :]
        row0 = 3 * (g * NSA_REP + r)
        gates = [_sigmoid(gate_ref[0, pl.ds(row0 + c, 1), :]) for c in range(3)]
        outs.append(gates[0] * o_c + gates[1] * o_s + gates[2] * o_w)
    o_ref[...] = jnp.concatenate(outs, axis=0).T.astype(o_ref.dtype)


def _gate_weights_t(w_gate):
    pad = HEAD_DIM - w_gate.shape[1]
    return jnp.pad(w_gate.T, ((0, pad), (0, 0))).astype(BF16)


def _key_augment(proj, onehot):
    t_len = proj.shape[1]
    n_sel = t_len // SEL_BLOCK
    pad = LANE - n_sel
    return jnp.concatenate(
        [proj[HN_KSL:HN_KSL + NSA_KV],
         jnp.zeros((NSA_KV, t_len, LANE - HEAD_DIM), BF16),
         jnp.broadcast_to(jnp.pad(onehot, ((0, 0), (0, pad)))[None], (NSA_KV, t_len, LANE))], axis=2)


def _slc_swa(proj, q_t, v_t, mask_t, k_aug, bias_win, rel_bias, gate_t, o_cmp_t):
    t_len = proj.shape[1]
    nt = t_len // TQ
    nkt = t_len // TK
    n_sel = t_len // SEL_BLOCK
    hw = NSA_REP * HEAD_DIM
    return pl.pallas_call(
        _slc_swa_kernel,
        grid=(NSA_KV, nt),
        in_specs=[pl.BlockSpec((1, hw, TQ), lambda g, i: (i, HT_NQ // NSA_REP + g, 0)),
                  pl.BlockSpec((1, n_sel, TQ), lambda g, i: (g, 0, i)),
                  pl.BlockSpec((1, t_len, 2 * LANE), lambda g, i: (g, 0, 0)),
                  pl.BlockSpec((nkt, HEAD_DIM, TK), lambda g, i: (0, HT_VSL + g, 0)),
                  pl.BlockSpec((1, t_len, HEAD_DIM), lambda g, i: (HN_KSW + g, 0, 0)),
                  pl.BlockSpec((nkt, HEAD_DIM, TK), lambda g, i: (0, HT_VSW + g, 0)),
                  pl.BlockSpec((NSA_REP, N_BIAS_SUB, BIAS_SUB, BIAS_SUB), lambda g, i: (g, 0, 0, 0)),
                  pl.BlockSpec(memory_space=pltpu.SMEM),
                  pl.BlockSpec((1, HEAD_DIM, TQ), lambda g, i: (i, 0, 0)),
                  pl.BlockSpec((hw, TQ), lambda g, i: (g, i))],
        out_specs=pl.BlockSpec((TQ, hw), lambda g, i: (i, g)),
        out_shape=jax.ShapeDtypeStruct((t_len, NSA_HEADS * HEAD_DIM), BF16),
        scratch_shapes=[pltpu.VMEM((NSA_REP, 2 * LANE, TQ), BF16),
                        pltpu.VMEM((NSA_REP, SUBLANE, TQ), F32),
                        pltpu.VMEM((NSA_REP, SUBLANE, TQ), F32),
                        pltpu.VMEM((NSA_REP, HEAD_DIM, TQ), F32),
                        pltpu.VMEM((2, NSA_REP, TK, TQ), F32),
                        pltpu.VMEM((NSA_REP, TK, TQ), BF16)],
        compiler_params=_cparams(("parallel", "arbitrary")),
        name="nsa_slc_swa",
    )(q_t, mask_t, k_aug, v_t, proj, v_t, bias_win, rel_bias, gate_t, o_cmp_t)


def _xattn_kernel(hb_ref, hf_ref, wq_ref, kv_ref, wo_ref, g_ref, b_ref, of_ref, ob_ref):
    q = _dot(hb_ref[...], wq_ref[...]).astype(BF16)
    scale = XA_HEAD_DIM ** -0.5
    outs = []
    for hh in range(XA_HEADS):
        lo = hh * XA_HEAD_DIM
        k = kv_ref[:, lo:lo + XA_HEAD_DIM]
        v = kv_ref[:, XA_W + lo:XA_W + lo + XA_HEAD_DIM]
        s = _dot_nt(q[:, lo:lo + XA_HEAD_DIM], k) * scale
        e = jnp.exp(s - jnp.max(s, axis=1, keepdims=True))
        p = e / jnp.sum(e, axis=1, keepdims=True)
        outs.append(_dot(p.astype(BF16), v).astype(BF16))
    o = jnp.concatenate(outs, axis=1)
    y = _layer_norm(ALPHA * hf_ref[...] + _dot(o, wo_ref[...]), g_ref[...], b_ref[...])
    of_ref[...] = y
    ob_ref[...] = y.astype(BF16)


def _xattn(hb, hf, wq, kv, wo, g, b, tm):
    m, n = hf.shape
    tm = min(tm, m)
    return pl.pallas_call(
        _xattn_kernel,
        grid=(m // tm,),
        in_specs=[pl.BlockSpec((tm, n), lambda i: (i, 0)),
                  pl.BlockSpec((tm, n), lambda i: (i, 0)),
                  pl.BlockSpec((n, XA_W), lambda i: (0, 0)),
                  pl.BlockSpec((N_MEM, 2 * XA_W), lambda i: (0, 0)),
                  pl.BlockSpec((XA_W, n), lambda i: (0, 0)),
                  pl.BlockSpec((1, n), lambda i: (0, 0)),
                  pl.BlockSpec((1, n), lambda i: (0, 0))],
        out_specs=[pl.BlockSpec((tm, n), lambda i: (i, 0)),
                   pl.BlockSpec((tm, n), lambda i: (i, 0))],
        out_shape=[jax.ShapeDtypeStruct((m, n), F32), jax.ShapeDtypeStruct((m, n), BF16)],
        compiler_params=_cparams(("parallel",)),
        name="xattn",
    )(hb, hf, wq, kv, wo, g.reshape(1, n), b.reshape(1, n))


def _ffn_up_kernel(h_ref, wa_ref, wg_ref, cwa_ref, cwg_ref, cba_ref, cbg_ref, o_ref, ta_ref, tg_ref):
    i = pl.program_id(1)

    @pl.when(i == 0)
    def _():
        ta_ref[...] = jnp.zeros_like(ta_ref)
        tg_ref[...] = jnp.zeros_like(tg_ref)

    hb = h_ref[...]
    tm = hb.shape[0]
    row = lax.broadcasted_iota(jnp.int32, (tm, o_ref.shape[1]), 0)

    def conv(w_ref, cw_ref, cb_ref, tail_ref):
        u = _dot(hb, w_ref[...])
        tail = tail_ref[...]
        u1 = jnp.where(row == 0, tail[SUBLANE - 1:SUBLANE, :], pltpu.roll(u, 1, 0))
        u2 = pltpu.roll(u, 2, 0)
        u2 = jnp.where(row == 0, tail[SUBLANE - 2:SUBLANE - 1, :], u2)
        u2 = jnp.where(row == 1, tail[SUBLANE - 1:SUBLANE, :], u2)
        tail_ref[...] = u[tm - SUBLANE:, :]
        return cw_ref[0:1, :] * u2 + cw_ref[1:2, :] * u1 + cw_ref[2:3, :] * u + cb_ref[...]

    a = conv(wa_ref, cwa_ref, cba_ref, ta_ref)
    gg = conv(wg_ref, cwg_ref, cbg_ref, tg_ref)
    o_ref[...] = (a * _gelu(gg)).astype(o_ref.dtype)


def _ffn_up(hb, w_up, conv_w, conv_b, tm, tn):
    m, k = hb.shape
    tm = min(tm, m)
    nj = D_FF // tn
    cb = conv_b.reshape(1, 2 * D_FF)
    return pl.pallas_call(
        _ffn_up_kernel,
        grid=(nj, m // tm),
        in_specs=[pl.BlockSpec((tm, k), lambda j, i: (i, 0)),
                  pl.BlockSpec((k, tn), lambda j, i: (0, j)),
                  pl.BlockSpec((k, tn), lambda j, i: (0, nj + j)),
                  pl.BlockSpec((3, tn), lambda j, i: (0, j)),
                  pl.BlockSpec((3, tn), lambda j, i: (0, nj + j)),
                  pl.BlockSpec((1, tn), lambda j, i: (0, j)),
                  pl.BlockSpec((1, tn), lambda j, i: (0, nj + j))],
        out_specs=pl.BlockSpec((tm, tn), lambda j, i: (i, j)),
        out_shape=jax.ShapeDtypeStruct((m, D_FF), BF16),
        scratch_shapes=[pltpu.VMEM((SUBLANE, tn), F32), pltpu.VMEM((SUBLANE, tn), F32)],
        compiler_params=_cparams(("parallel", "arbitrary")),
        name="ffn_up",
    )(hb, w_up, w_up, conv_w, conv_w, cb, cb)


def _static_tables(t_len):
    n16 = t_len // CMP_STRIDE
    n_sel = t_len // SEL_BLOCK
    n_cmp = (t_len - CMP_LEN) // CMP_STRIDE + 1
    cmp_start = np.arange(n16) * CMP_STRIDE
    cmp_end = cmp_start + CMP_LEN - 1
    sel = np.arange(n_sel)
    overlap_t = ((cmp_start[None, :] < (sel[:, None] + 1) * SEL_BLOCK)
                 & (cmp_end[None, :] >= sel[:, None] * SEL_BLOCK)
                 & (np.arange(n16)[None, :] < n_cmp))
    overlap_cat = np.concatenate([overlap_t, overlap_t], axis=1)
    tri_t = np.arange(TK)[None, :] > np.arange(TK)[:, None]
    tri_cat = np.concatenate([tri_t, tri_t], axis=1)
    onehot = (np.arange(t_len)[:, None] // SEL_BLOCK) == sel[None, :]
    return (jnp.asarray(overlap_cat, BF16), jnp.asarray(tri_cat, BF16), jnp.asarray(onehot, BF16))


def _split_in_proj(w):
    sizes = (S5_CH, 512, 512, 512, 1024, 256, 256, 256, 256, 256, 256, 3 * NSA_HEADS)
    offs = np.cumsum((0,) + sizes)
    u5, sbq, sbk, sbv, nq, kc, vc, ksl, vsl, ksw, vsw, gt = (
        w[:, int(offs[n]):int(offs[n + 1])] for n in range(len(sizes)))
    w_nat = jnp.concatenate([sbk, kc, vc, ksl, ksw], axis=1).astype(BF16)
    w_q = jnp.concatenate([sbq, nq], axis=1).T.astype(BF16)
    w_v = jnp.concatenate([sbv, vsl, vsw], axis=1).T.astype(BF16)
    return u5.astype(BF16), w_nat, w_q, w_v, _gate_weights_t(gt)


def kernel(x, mem, w_in, w_out, s5_lambda_re, s5_lambda_im, s5_log_dt, s5_b_re, s5_b_im, s5_c_re, s5_c_im, s5_d, s5_w_glu, s5_b_glu, nsa_cmp_pos, nsa_cmp_w1, nsa_cmp_w2, rel_bias, xa_wq, xa_wkv, xa_wo, ffn_w_up, ffn_conv_w, ffn_conv_b, ffn_w_down, ln_g, ln_b):
    t_len = x.shape[1]
    overlap_cat, tri_cat, onehot = _static_tables(t_len)
    bias_win, bias_cmp = _bias_tables(rel_bias)
    mem_b = mem[0].astype(BF16)
    hf = x[0]
    hb = hf.astype(BF16)
    for l in range(DEPTH):
        w_u5, w_nat, w_q, w_v, w_gate_t = _split_in_proj(w_in[l])
        u5 = _mm(hb, w_u5, F32, 512, 512)
        proj = _mm_heads(hb, w_nat, 512)
        q_t = _mm_nt(w_q, hb, BF16, 512, TQ)
        v_t = _mm_nt(w_v, hb, BF16, 512, TK)
        gate_t = _mm_nt(w_gate_t, hb, F32, HEAD_DIM, TQ)

        s5p = _s5_params(s5_lambda_re[l], s5_lambda_im[l], s5_log_dt[l], s5_b_re[l], s5_b_im[l],
                         s5_c_re[l], s5_c_im[l])
        y_s5 = _s5(u5, s5p, s5_w_glu[l], s5_b_glu[l], s5_d[l])
        y_sb = _stick_breaking(proj, q_t, v_t, tri_cat)

        kvc, kvc_t = _compress(proj, HN_KC, nsa_cmp_pos[l], nsa_cmp_w1[l], nsa_cmp_w2[l])
        o_cmp_t, mask_t = _cmp_select(q_t, kvc, kvc_t, bias_cmp, rel_bias, overlap_cat)
        y_nsa = _slc_swa(proj, q_t, v_t, mask_t, _key_augment(proj, onehot), bias_win, rel_bias,
                         gate_t, o_cmp_t)

        y = jnp.concatenate([y_s5, y_sb, y_nsa], axis=1)
        hf, hb = _mm_ln(y, w_out[l].astype(BF16), hf, ln_g[l, 0], ln_b[l, 0], 512, 512)

        kv = _mm(mem_b, xa_wkv[l].astype(BF16), BF16, N_MEM, 512)
        hf, hb = _xattn(hb, hf, xa_wq[l].astype(BF16), kv, xa_wo[l].astype(BF16),
                        ln_g[l, 1], ln_b[l, 1], 256)

        act = _ffn_up(hb, ffn_w_up[l].astype(BF16), ffn_conv_w[l], ffn_conv_b[l], 512, 512)
        hf, hb = _mm_ln(act, ffn_w_down[l].astype(BF16), hf, ln_g[l, 2], ln_b[l, 2], 512, 512)
    return hf[None]
```

```python
import functools
import math

import numpy as np
import jax
import jax.numpy as jnp
from jax import lax
from jax.experimental import pallas as pl
from jax.experimental.pallas import tpu as pltpu

F32 = jnp.float32
BF16 = jnp.bfloat16

D_MODEL = 2048
DEPTH = 4
HEAD_DIM = 64
S5_CH = 512
S5_GROUP = 16
S5_NG = 32
S5_P = 64
SB_HEADS = 8
NSA_HEADS = 16
NSA_KV = 4
NSA_REP = 4
CMP_LEN = 32
CMP_STRIDE = 16
CMP_HID = 128
SEL_BLOCK = 64
SEL_TOPN = 16
WINDOW = 512
FORCE_SCORE = 1e4
N_BUCKETS = 32
MAX_DIST = 1024
N_MEM = 256
XA_HEADS = 4
XA_HEAD_DIM = 128
XA_W = 512
D_FF = 5632
ALPHA = (2.0 * DEPTH) ** 0.25
LN_EPS = 1e-5
NEG = -1e30

LANE = 128
SUBLANE = 8
MXU_DIM = 256
VMEM_LIMIT = 56 * 1024 * 1024
TQ = 512
TK = 256
SUB_Q = TQ // LANE
SUB_K = TK // LANE
HEADS_PER_STEP = 4
EXP_ROWS = 32
S5_SEG = 64
S5_CHUNK = SUBLANE * S5_SEG
S5_CB = 8
BIAS_SUB = LANE
FAR_SUB = MAX_DIST // BIAS_SUB + 1
N_BIAS_SUB = FAR_SUB + SUB_K + SUB_Q - 2

HT_SBQ = 0
HT_NQ = HT_SBQ + SB_HEADS
HT_Q_ROWS = (HT_NQ + NSA_HEADS) * HEAD_DIM
HT_SBV = 0
HT_VSL = HT_SBV + SB_HEADS
HT_VSW = HT_VSL + NSA_KV
HT_V_ROWS = (HT_VSW + NSA_KV) * HEAD_DIM
HN_SBK = 0
HN_KC = HN_SBK + SB_HEADS
HN_KSL = HN_KC + 2 * NSA_KV
HN_KSW = HN_KSL + NSA_KV
HN_HEADS = HN_KSW + NSA_KV


def _cparams(sem):
    return pltpu.CompilerParams(dimension_semantics=sem, vmem_limit_bytes=VMEM_LIMIT)


def _dot(a, b):
    return jnp.dot(a, b, preferred_element_type=F32)


def _dot_nt(a, b):
    return lax.dot_general(a, b, (((1,), (1,)), ((), ())), preferred_element_type=F32)


def _split_bf16(x):
    hi = x.astype(BF16)
    lo = (x - hi.astype(F32)).astype(BF16)
    return hi, lo


def _gelu(x):
    c = math.sqrt(2.0 / math.pi)
    return 0.5 * x * (1.0 + jnp.tanh(c * (x + 0.044715 * (x * x * x))))


def _sigmoid(x):
    return 1.0 / (1.0 + jnp.exp(-x))


def _layer_norm(x, g, b):
    mu = jnp.mean(x, axis=-1, keepdims=True)
    xc = x - mu
    var = jnp.mean(xc * xc, axis=-1, keepdims=True)
    return xc * lax.rsqrt(var + LN_EPS) * g + b


def _mm_kernel(a_ref, b_ref, o_ref):
    o_ref[...] = _dot(a_ref[...], b_ref[...]).astype(o_ref.dtype)


def _mm(a, b, out_dtype, tm, tn):
    m, k = a.shape
    n = b.shape[1]
    tm = min(tm, m)
    return pl.pallas_call(
        _mm_kernel,
        grid=(n // tn, m // tm),
        in_specs=[pl.BlockSpec((tm, k), lambda j, i: (i, 0)),
                  pl.BlockSpec((k, tn), lambda j, i: (0, j))],
        out_specs=pl.BlockSpec((tm, tn), lambda j, i: (i, j)),
        out_shape=jax.ShapeDtypeStruct((m, n), out_dtype),
        compiler_params=_cparams(("parallel", "arbitrary")),
        name="mm",
    )(a, b)


def _mm_heads_kernel(a_ref, b_ref, o_ref):
    o = _dot(a_ref[...], b_ref[...]).astype(o_ref.dtype)
    for r in range(o_ref.shape[0]):
        o_ref[r] = o[:, r * HEAD_DIM:(r + 1) * HEAD_DIM]


def _mm_heads(a, b, tm):
    m, k = a.shape
    n = b.shape[1]
    tn = MXU_DIM
    hpt = tn // HEAD_DIM
    tm = min(tm, m)
    return pl.pallas_call(
        _mm_heads_kernel,
        grid=(n // tn, m // tm),
        in_specs=[pl.BlockSpec((tm, k), lambda j, i: (i, 0)),
                  pl.BlockSpec((k, tn), lambda j, i: (0, j))],
        out_specs=pl.BlockSpec((hpt, tm, HEAD_DIM), lambda j, i: (j, i, 0)),
        out_shape=jax.ShapeDtypeStruct((n // HEAD_DIM, m, HEAD_DIM), BF16),
        compiler_params=_cparams(("parallel", "arbitrary")),
        name="mm_heads",
    )(a, b)


def _mm_nt_kernel(w_ref, a_ref, o_ref):
    o_ref[0] = _dot_nt(w_ref[...], a_ref[...]).astype(o_ref.dtype)


def _mm_nt(w_t, a, out_dtype, tn, tm):
    n, k = w_t.shape
    m = a.shape[0]
    return pl.pallas_call(
        _mm_nt_kernel,
        grid=(n // tn, m // tm),
        in_specs=[pl.BlockSpec((tn, k), lambda j, i: (j, 0)),
                  pl.BlockSpec((tm, k), lambda j, i: (i, 0))],
        out_specs=pl.BlockSpec((1, tn, tm), lambda j, i: (i, j, 0)),
        out_shape=jax.ShapeDtypeStruct((m // tm, n, tm), out_dtype),
        compiler_params=_cparams(("parallel", "arbitrary")),
        name="mm_nt",
    )(w_t, a)


def _mm_ln_kernel(a_ref, w_ref, h_ref, g_ref, b_ref, of_ref, ob_ref, acc_ref):
    kk = pl.program_id(1)

    @pl.when(kk == 0)
    def _():
        acc_ref[...] = jnp.zeros_like(acc_ref)

    acc_ref[...] += _dot(a_ref[...], w_ref[...])

    @pl.when(kk == pl.num_programs(1) - 1)
    def _():
        y = _layer_norm(ALPHA * h_ref[...] + acc_ref[...], g_ref[...], b_ref[...])
        of_ref[...] = y
        ob_ref[...] = y.astype(BF16)


def _mm_ln(a, w, h, g, b, tm, tk):
    m, k = a.shape
    n = w.shape[1]
    tm = min(tm, m)
    return pl.pallas_call(
        _mm_ln_kernel,
        grid=(m // tm, k // tk),
        in_specs=[pl.BlockSpec((tm, tk), lambda i, kk: (i, kk)),
                  pl.BlockSpec((tk, n), lambda i, kk: (kk, 0)),
                  pl.BlockSpec((tm, n), lambda i, kk: (i, 0)),
                  pl.BlockSpec((1, n), lambda i, kk: (0, 0)),
                  pl.BlockSpec((1, n), lambda i, kk: (0, 0))],
        out_specs=[pl.BlockSpec((tm, n), lambda i, kk: (i, 0)),
                   pl.BlockSpec((tm, n), lambda i, kk: (i, 0))],
        out_shape=[jax.ShapeDtypeStruct((m, n), F32), jax.ShapeDtypeStruct((m, n), BF16)],
        scratch_shapes=[pltpu.VMEM((tm, n), F32)],
        compiler_params=_cparams(("parallel", "arbitrary")),
        name="mm_ln",
    )(a, w, h, g.reshape(1, n), b.reshape(1, n))


def _s5_kernel(u_ref, bm_ref, cm_ref, a_ref, d_ref, z_ref, s_ref, p_ref, carry_ref):
    tc = pl.program_id(1)
    w = s_ref.shape[1] // 2
    n_step = p_ref.shape[0]
    ar1 = a_ref[0:1, :]
    ai1 = a_ref[1:2, :]

    @pl.when(tc == 0)
    def _():
        carry_ref[...] = jnp.zeros_like(carry_ref)
        p_ref[0:1, 0:w] = ar1
        p_ref[0:1, w:2 * w] = ai1

        def pw(t, c):
            pr, pi = c
            nr = pr * ar1 - pi * ai1
            ni = pr * ai1 + pi * ar1
            p_ref[pl.ds(t, 1), 0:w] = nr
            p_ref[pl.ds(t, 1), w:2 * w] = ni
            return nr, ni

        lax.fori_loop(1, n_step, pw, (ar1, ai1))

    u = u_ref[...]
    s_ref[...] = _dot(u.astype(BF16), bm_ref[0])

    ar = jnp.broadcast_to(ar1, (SUBLANE, w))
    ai = jnp.broadcast_to(ai1, (SUBLANE, w))

    def step(t, c):
        sr, si = c
        r0 = pl.multiple_of(t * SUBLANE, SUBLANE)
        xr = s_ref[pl.ds(r0, SUBLANE), 0:w]
        xi = s_ref[pl.ds(r0, SUBLANE), w:2 * w]
        nr = ar * sr - ai * si + xr
        ni = ar * si + ai * sr + xi
        s_ref[pl.ds(r0, SUBLANE), 0:w] = nr
        s_ref[pl.ds(r0, SUBLANE), w:2 * w] = ni
        return nr, ni

    zero = jnp.zeros((SUBLANE, w), F32)
    fr, fi = lax.fori_loop(0, n_step, step, (zero, zero), unroll=8)

    amr = p_ref[n_step - 1:n_step, 0:w]
    ami = p_ref[n_step - 1:n_step, w:2 * w]
    row = lax.broadcasted_iota(jnp.int32, (SUBLANE, w), 0)
    cr = carry_ref[0:1, 0:w]
    ci = carry_ref[0:1, w:2 * w]
    cin_r = jnp.zeros((SUBLANE, w), F32)
    cin_i = jnp.zeros((SUBLANE, w), F32)
    for seg in range(SUBLANE):
        cin_r = jnp.where(row == seg, cr, cin_r)
        cin_i = jnp.where(row == seg, ci, cin_i)
        nr = fr[seg:seg + 1, :] + amr * cr - ami * ci
        ni = fi[seg:seg + 1, :] + amr * ci + ami * cr
        cr, ci = nr, ni
    carry_ref[0:1, 0:w] = cr
    carry_ref[0:1, w:2 * w] = ci

    def fix(t, c):
        r0 = pl.multiple_of(t * SUBLANE, SUBLANE)
        pr = p_ref[pl.ds(t, 1), 0:w]
        pi = p_ref[pl.ds(t, 1), w:2 * w]
        s_ref[pl.ds(r0, SUBLANE), 0:w] += pr * cin_r - pi * cin_i
        s_ref[pl.ds(r0, SUBLANE), w:2 * w] += pr * cin_i + pi * cin_r
        return c

    lax.fori_loop(0, n_step, fix, 0, unroll=8)

    y = _dot(s_ref[...].astype(BF16), cm_ref[0]) + d_ref[...] * u
    z_ref[...] = _gelu(y)


def _s5_glu_kernel(z_ref, w_ref, b_ref, o_ref):
    z = z_ref[...]
    gate = _sigmoid(_dot(z.astype(BF16), w_ref[...]) + b_ref[...])
    o_ref[...] = (z * gate).astype(o_ref.dtype)


def _s5_params(lam_re, lam_im, log_dt, b_re, b_im, c_re, c_im):
    delta = jnp.exp(log_dt)[:, None]
    mag = jnp.exp(lam_re * delta)
    ar = mag * jnp.cos(lam_im * delta)
    ai = mag * jnp.sin(lam_im * delta)
    den = lam_re * lam_re + lam_im * lam_im
    cr = ((ar - 1.0) * lam_re + ai * lam_im) / den
    ci = (ai * lam_re - (ar - 1.0) * lam_im) / den
    br = cr[..., None] * b_re - ci[..., None] * b_im
    bi = cr[..., None] * b_im + ci[..., None] * b_re
    ncb = S5_NG // S5_CB
    eye = jnp.eye(S5_CB, dtype=F32)

    def in_mat(b):
        b = b.reshape(ncb, S5_CB, S5_P, S5_GROUP)
        m = jnp.einsum('ngpc,gh->ngchp', b, eye)
        return m.reshape(ncb, S5_CB * S5_GROUP, S5_CB * S5_P)

    def out_mat(c):
        c = c.reshape(ncb, S5_CB, S5_GROUP, S5_P)
        m = jnp.einsum('ngcp,gh->ngphc', c, eye)
        return m.reshape(ncb, S5_CB * S5_P, S5_CB * S5_GROUP)

    bm = jnp.concatenate([in_mat(br), in_mat(bi)], axis=2).astype(BF16)
    cm = jnp.concatenate([out_mat(c_re), -out_mat(c_im)], axis=1).astype(BF16)
    a = jnp.stack([ar.reshape(-1), ai.reshape(-1)], axis=0)
    return bm, cm, a


def _s5(u, p, w_glu, b_glu, d):
    t_len = u.shape[0]
    nc = t_len // S5_CHUNK
    ncb = S5_NG // S5_CB
    wl = S5_CB * S5_GROUP
    ws = S5_CB * S5_P
    bm, cm, a = p
    up = u.reshape(nc, SUBLANE, S5_SEG, S5_CH).transpose(0, 2, 1, 3).reshape(t_len, S5_CH)
    z = pl.pallas_call(
        _s5_kernel,
        grid=(ncb, nc),
        in_specs=[pl.BlockSpec((S5_CHUNK, wl), lambda c, t: (t, c)),
                  pl.BlockSpec((1, wl, 2 * ws), lambda c, t: (c, 0, 0)),
                  pl.BlockSpec((1, 2 * ws, wl), lambda c, t: (c, 0, 0)),
                  pl.BlockSpec((2, ws), lambda c, t: (0, c)),
                  pl.BlockSpec((1, wl), lambda c, t: (0, c))],
        out_specs=pl.BlockSpec((S5_CHUNK, wl), lambda c, t: (t, c)),
        out_shape=jax.ShapeDtypeStruct((t_len, S5_CH), F32),
        scratch_shapes=[pltpu.VMEM((S5_CHUNK, 2 * ws), F32),
                        pltpu.VMEM((S5_SEG, 2 * ws), F32),
                        pltpu.VMEM((SUBLANE, 2 * ws), F32)],
        compiler_params=_cparams(("parallel", "arbitrary")),
        name="s5_scan",
    )(up, bm, cm, a, d.reshape(1, S5_CH))
    tm = min(1024, t_len)
    y = pl.pallas_call(
        _s5_glu_kernel,
        grid=(t_len // tm,),
        in_specs=[pl.BlockSpec((tm, S5_CH), lambda i: (i, 0)),
                  pl.BlockSpec((S5_CH, S5_CH), lambda i: (0, 0)),
                  pl.BlockSpec((1, S5_CH), lambda i: (0, 0))],
        out_specs=pl.BlockSpec((tm, S5_CH), lambda i: (i, 0)),
        out_shape=jax.ShapeDtypeStruct((t_len, S5_CH), BF16),
        compiler_params=_cparams(("parallel",)),
        name="s5_glu",
    )(z, w_glu.astype(BF16), b_glu.reshape(1, S5_CH))
    return y.reshape(nc, S5_SEG, SUBLANE, S5_CH).transpose(0, 2, 1, 3).reshape(t_len, S5_CH)


def _sb_kernel(q_ref, k_ref, v_ref, u_ref, o_ref, r_ref, acc_ref, nl_ref, arg_ref, w_ref):
    qi = pl.program_id(1)
    ucat = u_ref[...]
    qs = [q_ref[0, r * HEAD_DIM:(r + 1) * HEAD_DIM, :] * 0.125 for r in range(HEADS_PER_STEP)]
    r_ref[...] = jnp.zeros_like(r_ref)
    acc_ref[...] = jnp.zeros_like(acc_ref)
    key = lax.broadcasted_iota(jnp.int32, (TK, TQ), 0)
    qry = lax.broadcasted_iota(jnp.int32, (TK, TQ), 1)

    def tile(kt, diag_off):
        k0 = pl.multiple_of(kt * TK, TK)
        mask = None if diag_off is None else (key + diag_off) < qry
        sums = []
        for r in range(HEADS_PER_STEP):
            z = _dot(k_ref[r, pl.ds(k0, TK), :], qs[r])
            nl = jnp.maximum(z, 0.0) + jnp.log(1.0 + jnp.exp(-jnp.abs(z)))
            arg_ref[r] = z - nl
            if mask is not None:
                nl = jnp.where(mask, nl, 0.0)
            hi, lo = _split_bf16(nl)
            nl_ref[r, 0:TK, :] = hi
            nl_ref[r, TK:2 * TK, :] = lo
            sums.append(jnp.sum(nl, axis=0, keepdims=True))
        for r in range(HEADS_PER_STEP):
            later = _dot(ucat, nl_ref[r])
            w = jnp.exp(arg_ref[r] - later)
            if mask is not None:
                w = jnp.where(mask, w, 0.0)
            w_ref[r] = w.astype(BF16)
        for r in range(HEADS_PER_STEP):
            v = v_ref[kt, r * HEAD_DIM:(r + 1) * HEAD_DIM, :]
            r_old = r_ref[r, 0:1, :]
            acc_ref[r] += _dot(v, w_ref[r]) * jnp.exp(-r_old)
            r_ref[r, 0:1, :] = r_old + sums[r]

    n_diag = TQ // TK
    for c in reversed(range(n_diag)):
        tile(n_diag * qi + c, c * TK)

    def alive():
        return (jnp.max(jnp.exp(-r_ref[:, 0:1, :])) > 0.0).astype(jnp.int32)

    def cond(state):
        return (state[0] >= 0) & (state[1] > 0)

    def body(state):
        tile(state[0], None)
        return state[0] - 1, alive()

    lax.while_loop(cond, body, (n_diag * qi - 1, alive()))
    o_t = jnp.concatenate([acc_ref[r] for r in range(HEADS_PER_STEP)], axis=0)
    o_ref[...] = o_t.T.astype(o_ref.dtype)


def _stick_breaking(proj, q_t, v_t, ucat):
    t_len = proj.shape[1]
    nt = t_len // TQ
    hw = HEADS_PER_STEP * HEAD_DIM
    return pl.pallas_call(
        _sb_kernel,
        grid=(SB_HEADS // HEADS_PER_STEP, nt),
        in_specs=[pl.BlockSpec((1, hw, TQ), lambda h, i: (i, HT_SBQ // HEADS_PER_STEP + h, 0)),
                  pl.BlockSpec((HEADS_PER_STEP, t_len, HEAD_DIM),
                               lambda h, i: (HN_SBK // HEADS_PER_STEP + h, 0, 0)),
                  pl.BlockSpec((t_len // TK, hw, TK),
                               lambda h, i: (0, HT_SBV // HEADS_PER_STEP + h, 0)),
                  pl.BlockSpec((TK, 2 * TK), lambda h, i: (0, 0))],
        out_specs=pl.BlockSpec((TQ, hw), lambda h, i: (i, h)),
        out_shape=jax.ShapeDtypeStruct((t_len, SB_HEADS * HEAD_DIM), BF16),
        scratch_shapes=[pltpu.VMEM((HEADS_PER_STEP, SUBLANE, TQ), F32),
                        pltpu.VMEM((HEADS_PER_STEP, HEAD_DIM, TQ), F32),
                        pltpu.VMEM((HEADS_PER_STEP, 2 * TK, TQ), BF16),
                        pltpu.VMEM((HEADS_PER_STEP, TK, TQ), F32),
                        pltpu.VMEM((HEADS_PER_STEP, TK, TQ), BF16)],
        compiler_params=_cparams(("parallel", "arbitrary")),
        name="stick_breaking",
    )(q_t, proj, v_t, ucat)


def _t5_bucket(dist):
    n = jnp.maximum(dist, 0)
    max_exact = N_BUCKETS // 2
    nf = jnp.maximum(n, 1).astype(jnp.float32)
    large = max_exact + (jnp.log(nf / max_exact) / math.log(MAX_DIST / max_exact)
                         * (N_BUCKETS - max_exact)).astype(jnp.int32)
    large = jnp.minimum(large, N_BUCKETS - 1)
    return jnp.where(n < max_exact, n, large)


def _bias_lookup_kernel(tab_ref, idx_ref, o_ref):
    h = pl.program_id(0)
    idx = idx_ref[...]
    acc = jnp.zeros(idx.shape, F32)
    for b in range(N_BUCKETS):
        acc = jnp.where(idx == b, tab_ref[b, h], acc)
    o_ref[0] = acc


def _bias_lookup(tab, idx):
    r, c = idx.shape
    n_h = tab.shape[1]
    return pl.pallas_call(
        _bias_lookup_kernel,
        grid=(n_h,),
        in_specs=[pl.BlockSpec(memory_space=pltpu.SMEM),
                  pl.BlockSpec((r, c), lambda h: (0, 0))],
        out_specs=pl.BlockSpec((1, r, c), lambda h: (h, 0, 0)),
        out_shape=jax.ShapeDtypeStruct((n_h, r, c), F32),
        compiler_params=_cparams(("arbitrary",)),
        name="bias_lookup",
    )(tab, idx)


def _bias_tables(rel_bias):
    i = jnp.arange(BIAS_SUB)
    d_win = (BIAS_SUB * jnp.arange(N_BIAS_SUB)[:, None, None] + i[None, None, :] - i[None, :, None])
    idx_win = _t5_bucket(d_win).reshape(N_BIAS_SUB * BIAS_SUB, BIAS_SUB)
    win = _bias_lookup(rel_bias, idx_win).reshape(NSA_HEADS, N_BIAS_SUB, BIAS_SUB, BIAS_SUB)
    n_m = (LANE * CMP_STRIDE) // TQ
    m = jnp.arange(n_m)[:, None, None]
    c = jnp.arange(2 * LANE)[None, :, None]
    qi = jnp.arange(TQ)[None, None, :]
    d_cmp = TQ * m + qi + LANE * CMP_STRIDE - CMP_STRIDE * c - (CMP_LEN - 1)
    idx_cmp = _t5_bucket(d_cmp).reshape(n_m * 2 * LANE, TQ)
    cmp_t = _bias_lookup(rel_bias, idx_cmp).reshape(NSA_HEADS, n_m, 2 * LANE, TQ)
    return win, cmp_t


def _compress_kernel(c0_ref, c1_ref, w1_ref, pos_ref, w2_ref, w2t_ref, o_ref, ot_ref):
    half = c0_ref.shape[3]
    w1 = w1_ref[0]
    posb = _dot(pos_ref[0], w1)[0:1, :]
    hid = _dot(c0_ref[0, 0], w1[0:half, :]) + _dot(c1_ref[0, 0], w1[half:, :]) + posb
    act = _gelu(hid).astype(BF16)
    o_ref[0, 0] = _dot(act, w2_ref[0]).astype(o_ref.dtype)
    ot_ref[0, 0] = _dot_nt(w2t_ref[0], act).astype(ot_ref.dtype)


def _compress(proj, head0, cmp_pos, cmp_w1, cmp_w2):
    t_len = proj.shape[1]
    n16 = t_len // CMP_STRIDE
    half = CMP_STRIDE * HEAD_DIM
    kv = proj[head0:head0 + 2 * NSA_KV].reshape(2, NSA_KV, n16, half)
    kv_next = jnp.concatenate([kv[:, :, 1:], jnp.zeros((2, NSA_KV, 1, half), BF16)], axis=2)
    pos = jnp.broadcast_to(cmp_pos.reshape(2, 1, CMP_LEN * HEAD_DIM), (2, SUBLANE, CMP_LEN * HEAD_DIM))
    w2 = cmp_w2.astype(BF16)
    return pl.pallas_call(
        _compress_kernel,
        grid=(2, NSA_KV),
        in_specs=[pl.BlockSpec((1, 1, n16, half), lambda j, g: (j, g, 0, 0)),
                  pl.BlockSpec((1, 1, n16, half), lambda j, g: (j, g, 0, 0)),
                  pl.BlockSpec((1, 2 * half, CMP_HID), lambda j, g: (j, 0, 0)),
                  pl.BlockSpec((1, SUBLANE, 2 * half), lambda j, g: (j, 0, 0)),
                  pl.BlockSpec((1, CMP_HID, HEAD_DIM), lambda j, g: (j, 0, 0)),
                  pl.BlockSpec((1, HEAD_DIM, CMP_HID), lambda j, g: (j, 0, 0))],
        out_specs=[pl.BlockSpec((1, 1, n16, HEAD_DIM), lambda j, g: (j, g, 0, 0)),
                   pl.BlockSpec((1, 1, HEAD_DIM, n16), lambda j, g: (j, g, 0, 0))],
        out_shape=[jax.ShapeDtypeStruct((2, NSA_KV, n16, HEAD_DIM), BF16),
                   jax.ShapeDtypeStruct((2, NSA_KV, HEAD_DIM, n16), BF16)],
        compiler_params=_cparams(("parallel", "arbitrary")),
        name="nsa_compress",
    )(kv, kv_next, cmp_w1.astype(BF16), pos.astype(BF16), w2, w2.transpose(0, 2, 1))


def _cmp_select_kernel(q_ref, kc_ref, vct_ref, bias_ref, tab_ref, ov_ref, oc_ref, mn_ref, *, n_cmp):
    g = pl.program_id(0)
    qi = pl.program_id(1)
    n16 = kc_ref.shape[2]
    n_ct = n16 // LANE
    kc = kc_ref[0, 0]
    vct = vct_ref[0, 0]
    q_per_ct = (LANE * CMP_STRIDE) // TQ
    jd = qi // q_per_ct
    n = lax.broadcasted_iota(jnp.int32, (n16, TQ), 0)
    t = qi * TQ + lax.broadcasted_iota(jnp.int32, (n16, TQ), 1)
    valid = (CMP_STRIDE * n + (CMP_LEN - 1) <= t) & (n < n_cmp)
    psum = jnp.zeros((n16, TQ), F32)
    for r in range(NSA_REP):
        q = q_ref[0, r * HEAD_DIM:(r + 1) * HEAD_DIM, :] * 0.125
        far = tab_ref[N_BUCKETS - 1, g * NSA_REP + r]
        rows = []
        for jc in range(n_ct):
            near = jnp.where(jc == jd, bias_ref[r, 0, LANE:2 * LANE, :], bias_ref[r, 0, 0:LANE, :])
            rows.append(jnp.where((jc == jd) | (jc == jd - 1), near, far))
        bias = rows[0] if n_ct == 1 else jnp.concatenate(rows, axis=0)
        s = jnp.where(valid, _dot(kc, q) + bias, NEG)
        mx = jnp.max(s, axis=0, keepdims=True)
        e = jnp.where(valid, jnp.exp(s - mx), 0.0)
        p = e * (1.0 / jnp.maximum(jnp.sum(e, axis=0, keepdims=True), 1e-30))
        psum = psum + p
        oc_ref[r * HEAD_DIM:(r + 1) * HEAD_DIM, :] = _dot(vct, p.astype(BF16))
    hi, lo = _split_bf16(psum)
    imp = _dot(ov_ref[...], jnp.concatenate([hi, lo], axis=0))
    n_sel = imp.shape[0]
    j = lax.broadcasted_iota(jnp.int32, (n_sel, LANE), 0)
    jf = j.astype(F32)
    for cc in range(TQ // LANE):
        tq = qi * TQ + cc * LANE + lax.broadcasted_iota(jnp.int32, (n_sel, LANE), 1)
        cur = tq // SEL_BLOCK
        forced = (j == 0) | (j == cur) | (j == cur - 1)
        score = jnp.where(j <= cur,
                          jnp.where(forced, FORCE_SCORE, imp[:, cc * LANE:(cc + 1) * LANE]), -1.0)
        sel = jnp.zeros((n_sel, LANE), F32)
        for _ in range(min(SEL_TOPN, n_sel)):
            mx = jnp.max(score, axis=0, keepdims=True)
            first = jnp.min(jnp.where(score == mx, jf, float(n_sel)), axis=0, keepdims=True)
            hit = jf == first
            sel = jnp.where(hit, 1.0, sel)
            score = jnp.where(hit, -jnp.inf, score)
        mn_ref[0, :, cc * LANE:(cc + 1) * LANE] = (sel - 1.0).astype(mn_ref.dtype)


def _cmp_select(q_t, kc, vct, bias_cmp, rel_bias, overlap_cat):
    nt = q_t.shape[0]
    t_len = nt * TQ
    n16 = kc.shape[2]
    n_sel = t_len // SEL_BLOCK
    n_cmp = (t_len - CMP_LEN) // CMP_STRIDE + 1
    q_per_ct = (LANE * CMP_STRIDE) // TQ
    hw = NSA_REP * HEAD_DIM
    return pl.pallas_call(
        functools.partial(_cmp_select_kernel, n_cmp=n_cmp),
        grid=(NSA_KV, nt),
        in_specs=[pl.BlockSpec((1, hw, TQ), lambda g, i: (i, HT_NQ // NSA_REP + g, 0)),
                  pl.BlockSpec((1, 1, n16, HEAD_DIM), lambda g, i: (0, g, 0, 0)),
                  pl.BlockSpec((1, 1, HEAD_DIM, n16), lambda g, i: (1, g, 0, 0)),
                  pl.BlockSpec((NSA_REP, 1, 2 * LANE, TQ), lambda g, i: (g, i % q_per_ct, 0, 0)),
                  pl.BlockSpec(memory_space=pltpu.SMEM),
                  pl.BlockSpec((n_sel, 2 * n16), lambda g, i: (0, 0))],
        out_specs=[pl.BlockSpec((hw, TQ), lambda g, i: (g, i)),
                   pl.BlockSpec((1, n_sel, TQ), lambda g, i: (g, 0, i))],
        out_shape=[jax.ShapeDtypeStruct((NSA_HEADS * HEAD_DIM, t_len), F32),
                   jax.ShapeDtypeStruct((NSA_KV, n_sel, t_len), BF16)],
        compiler_params=_cparams(("parallel", "arbitrary")),
        name="nsa_cmp_select",
    )(q_t, kc, vct, bias_cmp, rel_bias, overlap_cat)


def _bias_tile(win_ref, r, base):
    rows = []
    for b in range(SUB_K):
        cols = []
        for a in range(SUB_Q):
            idx = jnp.maximum(base + a - b, 0)
            cols.append(win_ref[r, idx])
        rows.append(jnp.concatenate(cols, axis=1))
    return jnp.concatenate(rows, axis=0)


def _slc_swa_kernel(q_ref, mn_ref, ka_ref, vs_ref, kw_ref, vw_ref, win_ref, tab_ref, gate_ref,
                    oc_ref, o_ref, qa_ref, m_ref, l_ref, acc_ref, s_ref, p_ref):
    g = pl.program_id(0)
    qi = pl.program_id(1)
    n_sel = mn_ref.shape[1]
    n_diag = TQ // TK
    big = mn_ref[0] * jnp.asarray(1e30, BF16)
    for r in range(NSA_REP):
        qa_ref[r, 0:HEAD_DIM, :] = q_ref[0, r * HEAD_DIM:(r + 1) * HEAD_DIM, :] * 0.125
        qa_ref[r, HEAD_DIM:LANE, :] = jnp.zeros((LANE - HEAD_DIM, TQ), BF16)
        qa_ref[r, LANE:LANE + n_sel, :] = big
        if n_sel < LANE:
            qa_ref[r, LANE + n_sel:2 * LANE, :] = jnp.zeros((LANE - n_sel, TQ), BF16)
    m_ref[...] = jnp.full_like(m_ref, NEG)
    l_ref[...] = jnp.zeros_like(l_ref)
    acc_ref[...] = jnp.zeros_like(acc_ref)
    key = lax.broadcasted_iota(jnp.int32, (TK, TQ), 0)
    qry = lax.broadcasted_iota(jnp.int32, (TK, TQ), 1)

    def scores(kt, far, diag_off, buf):
        k0 = pl.multiple_of(kt * TK, TK)
        ka = ka_ref[0, pl.ds(k0, TK), :]
        for r in range(NSA_REP):
            s = _dot(ka, qa_ref[r])
            if not far:
                s = s + _bias_tile(win_ref, r, SUB_Q * qi - SUB_K * kt)
            if diag_off is not None:
                s = jnp.where(key + diag_off <= qry, s, NEG)
            s_ref[buf, r] = s

    def consume(kt, buf):
        v = vs_ref[kt]
        stats = []
        groups = EXP_ROWS // SUBLANE

        def rows8(c0):
            return s_ref[buf, r, c0:c0 + EXP_ROWS, :].reshape(groups, SUBLANE, TQ)

        for r in range(NSA_REP):
            mx8 = jnp.max(rows8(0), axis=0)
            for c0 in range(EXP_ROWS, TK, EXP_ROWS):
                mx8 = jnp.maximum(mx8, jnp.max(rows8(c0), axis=0))
            m_old = m_ref[r, 0:1, :]
            m_new = jnp.maximum(m_old, jnp.max(mx8, axis=0, keepdims=True))
            m_ref[r, 0:1, :] = m_new
            stats.append((m_new, jnp.exp(m_old - m_new)))
        for r in range(NSA_REP):
            m_new, alpha = stats[r]
            ps8 = jnp.zeros((SUBLANE, TQ), F32)
            for c0 in range(0, TK, EXP_ROWS):
                p = jnp.exp(s_ref[buf, r, c0:c0 + EXP_ROWS, :] - m_new)
                ps8 = ps8 + jnp.sum(p.reshape(groups, SUBLANE, TQ), axis=0)
                p_ref[r, c0:c0 + EXP_ROWS, :] = p.astype(BF16)
            l_ref[r, 0:1, :] = alpha * l_ref[r, 0:1, :] + jnp.sum(ps8, axis=0, keepdims=True)
        for r in range(NSA_REP):
            acc_ref[r] = stats[r][1] * acc_ref[r] + _dot(v, p_ref[r])

    def slc_tile(kt, far, diag_off):
        scores(kt, far, diag_off, 0)
        consume(kt, 0)

    n_far = jnp.maximum((SUB_Q * qi - (SUB_K - 1) - FAR_SUB) // SUB_K + 1, 0)

    scores(0, True, None, 0)

    def far_body(j, c):
        scores(2 * j + 1, True, None, 1)
        consume(2 * j, 0)
        scores(2 * j + 2, True, None, 0)
        consume(2 * j + 1, 1)
        return c

    lax.fori_loop(0, n_far // 2, far_body, 0)

    @pl.when(n_far % 2 == 1)
    def _():
        consume(n_far - 1, 0)
    for r in range(NSA_REP):
        m_ref[r, 0:1, :] = m_ref[r, 0:1, :] + tab_ref[N_BUCKETS - 1, g * NSA_REP + r]

    def band_body(kt, c):
        slc_tile(kt, False, None)
        return c

    lax.fori_loop(n_far, n_diag * qi, band_body, 0)
    for c in range(n_diag):
        slc_tile(n_diag * qi + c, False, c * TK)

    n_wt = WINDOW // TK + n_diag
    outs = []
    for r in range(NSA_REP):
        q = qa_ref[r, 0:HEAD_DIM, :]
        ss, vv, oks = [], [], []
        for c in range(n_wt):
            back = WINDOW // TK - c
            kt = n_diag * qi - back
            ktc = jnp.maximum(kt, 0)
            k0 = pl.multiple_of(ktc * TK, TK)
            s = _dot(kw_ref[0, pl.ds(k0, TK), :], q) + _bias_tile(win_ref, r, SUB_K * back)
            dist = back * TK + qry - key
            ok = (dist >= 0) & (dist < WINDOW) & (kt >= 0)
            ss.append(jnp.where(ok, s, NEG))
            oks.append(ok)
            vv.append(vw_ref[ktc])
        mx = jnp.max(ss[0], axis=0, keepdims=True)
        for s in ss[1:]:
            mx = jnp.maximum(mx, jnp.max(s, axis=0, keepdims=True))
        den = jnp.zeros((1, TQ), F32)
        o_w = jnp.zeros((HEAD_DIM, TQ), F32)
        for s, v, ok in zip(ss, vv, oks):
            e = jnp.where(ok, jnp.exp(s - mx), 0.0)
            den = den + jnp.sum(e, axis=0, keepdims=True)
            o_w = o_w + _dot(v, e.astype(BF16))
        o_w = o_w * (1.0 / jnp.maximum(den, 1e-30))
        o_s = acc_ref[r] * (1.0 / jnp.maximum(l_ref[r, 0:1, :], 1e-30))
        o_c = oc_ref[r * HEAD_DIM:(r + 1) * HEAD_DIM, :]
        row0 = 3 * (g * NSA_REP + r)
        gates = [_sigmoid(gate_ref[0, pl.ds(row0 + c, 1), :]) for c in range(3)]
        outs.append(gates[0] * o_c + gates[1] * o_s + gates[2] * o_w)
    o_ref[...] = jnp.concatenate(outs, axis=0).T.astype(o_ref.dtype)


def _gate_weights_t(w_gate):
    pad = HEAD_DIM - w_gate.shape[1]
    return jnp.pad(w_gate.T, ((0, pad), (0, 0))).astype(BF16)


def _key_augment(proj, onehot):
    t_len = proj.shape[1]
    n_sel = t_len // SEL_BLOCK
    pad = LANE - n_sel
    return jnp.concatenate(
        [proj[HN_KSL:HN_KSL + NSA_KV],
         jnp.zeros((NSA_KV, t_len, LANE - HEAD_DIM), BF16),
         jnp.broadcast_to(jnp.pad(onehot, ((0, 0), (0, pad)))[None], (NSA_KV, t_len, LANE))], axis=2)


def _slc_swa(proj, q_t, v_t, mask_t, k_aug, bias_win, rel_bias, gate_t, o_cmp_t):
    t_len = proj.shape[1]
    nt = t_len // TQ
    nkt = t_len // TK
    n_sel = t_len // SEL_BLOCK
    hw = NSA_REP * HEAD_DIM
    return pl.pallas_call(
        _slc_swa_kernel,
        grid=(NSA_KV, nt),
        in_specs=[pl.BlockSpec((1, hw, TQ), lambda g, i: (i, HT_NQ // NSA_REP + g, 0)),
                  pl.BlockSpec((1, n_sel, TQ), lambda g, i: (g, 0, i)),
                  pl.BlockSpec((1, t_len, 2 * LANE), lambda g, i: (g, 0, 0)),
                  pl.BlockSpec((nkt, HEAD_DIM, TK), lambda g, i: (0, HT_VSL + g, 0)),
                  pl.BlockSpec((1, t_len, HEAD_DIM), lambda g, i: (HN_KSW + g, 0, 0)),
                  pl.BlockSpec((nkt, HEAD_DIM, TK), lambda g, i: (0, HT_VSW + g, 0)),
                  pl.BlockSpec((NSA_REP, N_BIAS_SUB, BIAS_SUB, BIAS_SUB), lambda g, i: (g, 0, 0, 0)),
                  pl.BlockSpec(memory_space=pltpu.SMEM),
                  pl.BlockSpec((1, HEAD_DIM, TQ), lambda g, i: (i, 0, 0)),
                  pl.BlockSpec((hw, TQ), lambda g, i: (g, i))],
        out_specs=pl.BlockSpec((TQ, hw), lambda g, i: (i, g)),
        out_shape=jax.ShapeDtypeStruct((t_len, NSA_HEADS * HEAD_DIM), BF16),
        scratch_shapes=[pltpu.VMEM((NSA_REP, 2 * LANE, TQ), BF16),
                        pltpu.VMEM((NSA_REP, SUBLANE, TQ), F32),
                        pltpu.VMEM((NSA_REP, SUBLANE, TQ), F32),
                        pltpu.VMEM((NSA_REP, HEAD_DIM, TQ), F32),
                        pltpu.VMEM((2, NSA_REP, TK, TQ), F32),
                        pltpu.VMEM((NSA_REP, TK, TQ), BF16)],
        compiler_params=_cparams(("parallel", "arbitrary")),
        name="nsa_slc_swa",
    )(q_t, mask_t, k_aug, v_t, proj, v_t, bias_win, rel_bias, gate_t, o_cmp_t)


def _xattn_kernel(hb_ref, hf_ref, wq_ref, kv_ref, wo_ref, g_ref, b_ref, of_ref, ob_ref):
    q = _dot(hb_ref[...], wq_ref[...]).astype(BF16)
    scale = XA_HEAD_DIM ** -0.5
    outs = []
    for hh in range(XA_HEADS):
        lo = hh * XA_HEAD_DIM
        k = kv_ref[:, lo:lo + XA_HEAD_DIM]
        v = kv_ref[:, XA_W + lo:XA_W + lo + XA_HEAD_DIM]
        s = _dot_nt(q[:, lo:lo + XA_HEAD_DIM], k) * scale
        e = jnp.exp(s - jnp.max(s, axis=1, keepdims=True))
        p = e / jnp.sum(e, axis=1, keepdims=True)
        outs.append(_dot(p.astype(BF16), v).astype(BF16))
    o = jnp.concatenate(outs, axis=1)
    y = _layer_norm(ALPHA * hf_ref[...] + _dot(o, wo_ref[...]), g_ref[...], b_ref[...])
    of_ref[...] = y
    ob_ref[...] = y.astype(BF16)


def _xattn(hb, hf, wq, kv, wo, g, b, tm):
    m, n = hf.shape
    tm = min(tm, m)
    return pl.pallas_call(
        _xattn_kernel,
        grid=(m // tm,),
        in_specs=[pl.BlockSpec((tm, n), lambda i: (i, 0)),
                  pl.BlockSpec((tm, n), lambda i: (i, 0)),
                  pl.BlockSpec((n, XA_W), lambda i: (0, 0)),
                  pl.BlockSpec((N_MEM, 2 * XA_W), lambda i: (0, 0)),
                  pl.BlockSpec((XA_W, n), lambda i: (0, 0)),
                  pl.BlockSpec((1, n), lambda i: (0, 0)),
                  pl.BlockSpec((1, n), lambda i: (0, 0))],
        out_specs=[pl.BlockSpec((tm, n), lambda i: (i, 0)),
                   pl.BlockSpec((tm, n), lambda i: (i, 0))],
        out_shape=[jax.ShapeDtypeStruct((m, n), F32), jax.ShapeDtypeStruct((m, n), BF16)],
        compiler_params=_cparams(("parallel",)),
        name="xattn",
    )(hb, hf, wq, kv, wo, g.reshape(1, n), b.reshape(1, n))


def _ffn_up_kernel(h_ref, wa_ref, wg_ref, cwa_ref, cwg_ref, cba_ref, cbg_ref, o_ref, ta_ref, tg_ref):
    i = pl.program_id(1)

    @pl.when(i == 0)
    def _():
        ta_ref[...] = jnp.zeros_like(ta_ref)
        tg_ref[...] = jnp.zeros_like(tg_ref)

    hb = h_ref[...]
    tm = hb.shape[0]
    row = lax.broadcasted_iota(jnp.int32, (tm, o_ref.shape[1]), 0)

    def conv(w_ref, cw_ref, cb_ref, tail_ref):
        u = _dot(hb, w_ref[...])
        tail = tail_ref[...]
        u1 = jnp.where(row == 0, tail[SUBLANE - 1:SUBLANE, :], pltpu.roll(u, 1, 0))
        u2 = pltpu.roll(u, 2, 0)
        u2 = jnp.where(row == 0, tail[SUBLANE - 2:SUBLANE - 1, :], u2)
        u2 = jnp.where(row == 1, tail[SUBLANE - 1:SUBLANE, :], u2)
        tail_ref[...] = u[tm - SUBLANE:, :]
        return cw_ref[0:1, :] * u2 + cw_ref[1:2, :] * u1 + cw_ref[2:3, :] * u + cb_ref[...]

    a = conv(wa_ref, cwa_ref, cba_ref, ta_ref)
    gg = conv(wg_ref, cwg_ref, cbg_ref, tg_ref)
    o_ref[...] = (a * _gelu(gg)).astype(o_ref.dtype)


def _ffn_up(hb, w_up, conv_w, conv_b, tm, tn):
    m, k = hb.shape
    tm = min(tm, m)
    nj = D_FF // tn
    cb = conv_b.reshape(1, 2 * D_FF)
    return pl.pallas_call(
        _ffn_up_kernel,
        grid=(nj, m // tm),
        in_specs=[pl.BlockSpec((tm, k), lambda j, i: (i, 0)),
                  pl.BlockSpec((k, tn), lambda j, i: (0, j)),
                  pl.BlockSpec((k, tn), lambda j, i: (0, nj + j)),
                  pl.BlockSpec((3, tn), lambda j, i: (0, j)),
                  pl.BlockSpec((3, tn), lambda j, i: (0, nj + j)),
                  pl.BlockSpec((1, tn), lambda j, i: (0, j)),
                  pl.BlockSpec((1, tn), lambda j, i: (0, nj + j))],
        out_specs=pl.BlockSpec((tm, tn), lambda j, i: (i, j)),
        out_shape=jax.ShapeDtypeStruct((m, D_FF), BF16),
        scratch_shapes=[pltpu.VMEM((SUBLANE, tn), F32), pltpu.VMEM((SUBLANE, tn), F32)],
        compiler_params=_cparams(("parallel", "arbitrary")),
        name="ffn_up",
    )(hb, w_up, w_up, conv_w, conv_w, cb, cb)


def _static_tables(t_len):
    n16 = t_len // CMP_STRIDE
    n_sel = t_len // SEL_BLOCK
    n_cmp = (t_len - CMP_LEN) // CMP_STRIDE + 1
    cmp_start = np.arange(n16) * CMP_STRIDE
    cmp_end = cmp_start + CMP_LEN - 1
    sel = np.arange(n_sel)
    overlap_t = ((cmp_start[None, :] < (sel[:, None] + 1) * SEL_BLOCK)
                 & (cmp_end[None, :] >= sel[:, None] * SEL_BLOCK)
                 & (np.arange(n16)[None, :] < n_cmp))
    overlap_cat = np.concatenate([overlap_t, overlap_t], axis=1)
    tri_t = np.arange(TK)[None, :] > np.arange(TK)[:, None]
    tri_cat = np.concatenate([tri_t, tri_t], axis=1)
    onehot = (np.arange(t_len)[:, None] // SEL_BLOCK) == sel[None, :]
    return (jnp.asarray(overlap_cat, BF16), jnp.asarray(tri_cat, BF16), jnp.asarray(onehot, BF16))


def _split_in_proj(w):
    sizes = (S5_CH, 512, 512, 512, 1024, 256, 256, 256, 256, 256, 256, 3 * NSA_HEADS)
    offs = np.cumsum((0,) + sizes)
    u5, sbq, sbk, sbv, nq, kc, vc, ksl, vsl, ksw, vsw, gt = (
        w[:, int(offs[n]):int(offs[n + 1])] for n in range(len(sizes)))
    w_nat = jnp.concatenate([sbk, kc, vc, ksl, ksw], axis=1).astype(BF16)
    w_q = jnp.concatenate([sbq, nq], axis=1).T.astype(BF16)
    w_v = jnp.concatenate([sbv, vsl, vsw], axis=1).T.astype(BF16)
    return u5.astype(BF16), w_nat, w_q, w_v, _gate_weights_t(gt)


def kernel(x, mem, w_in, w_out, s5_lambda_re, s5_lambda_im, s5_log_dt, s5_b_re, s5_b_im, s5_c_re, s5_c_im, s5_d, s5_w_glu, s5_b_glu, nsa_cmp_pos, nsa_cmp_w1, nsa_cmp_w2, rel_bias, xa_wq, xa_wkv, xa_wo, ffn_w_up, ffn_conv_w, ffn_conv_b, ffn_w_down, ln_g, ln_b):
    t_len = x.shape[1]
    overlap_cat, tri_cat, onehot = _static_tables(t_len)
    bias_win, bias_cmp = _bias_tables(rel_bias)
    mem_b = mem[0].astype(BF16)
    hf = x[0]
    hb = hf.astype(BF16)
    for l in range(DEPTH):
        w_u5, w_nat, w_q, w_v, w_gate_t = _split_in_proj(w_in[l])
        u5 = _mm(hb, w_u5, F32, 512, 512)
        proj = _mm_heads(hb, w_nat, 512)
        q_t = _mm_nt(w_q, hb, BF16, 512, TQ)
        v_t = _mm_nt(w_v, hb, BF16, 512, TK)
        gate_t = _mm_nt(w_gate_t, hb, F32, HEAD_DIM, TQ)

        s5p = _s5_params(s5_lambda_re[l], s5_lambda_im[l], s5_log_dt[l], s5_b_re[l], s5_b_im[l],
                         s5_c_re[l], s5_c_im[l])
        y_s5 = _s5(u5, s5p, s5_w_glu[l], s5_b_glu[l], s5_d[l])
        y_sb = _stick_breaking(proj, q_t, v_t, tri_cat)

        kvc, kvc_t = _compress(proj, HN_KC, nsa_cmp_pos[l], nsa_cmp_w1[l], nsa_cmp_w2[l])
        o_cmp_t, mask_t = _cmp_select(q_t, kvc, kvc_t, bias_cmp, rel_bias, overlap_cat)
        y_nsa = _slc_swa(proj, q_t, v_t, mask_t, _key_augment(proj, onehot), bias_win, rel_bias,
                         gate_t, o_cmp_t)

        y = jnp.concatenate([y_s5, y_sb, y_nsa], axis=1)
        hf, hb = _mm_ln(y, w_out[l].astype(BF16), hf, ln_g[l, 0], ln_b[l, 0], 512, 1024)

        kv = _mm(mem_b, xa_wkv[l].astype(BF16), BF16, N_MEM, 512)
        hf, hb = _xattn(hb, hf, xa_wq[l].astype(BF16), kv, xa_wo[l].astype(BF16),
                        ln_g[l, 1], ln_b[l, 1], 256)

        act = _ffn_up(hb, ffn_w_up[l].astype(BF16), ffn_conv_w[l], ffn_conv_b[l], 1024, 512)
        hf, hb = _mm_ln(act, ffn_w_down[l].astype(BF16), hf, ln_g[l, 2], ln_b[l, 2], 512, 512)
    return hf[None]
```

```python
import functools
import math

import numpy as np
import jax
import jax.numpy as jnp
from jax import lax
from jax.experimental import pallas as pl
from jax.experimental.pallas import tpu as pltpu

F32 = jnp.float32
BF16 = jnp.bfloat16

D_MODEL = 2048
DEPTH = 4
HEAD_DIM = 64
S5_CH = 512
S5_GROUP = 16
S5_NG = 32
S5_P = 64
SB_HEADS = 8
NSA_HEADS = 16
NSA_KV = 4
NSA_REP = 4
CMP_LEN = 32
CMP_STRIDE = 16
CMP_HID = 128
SEL_BLOCK = 64
SEL_TOPN = 16
WINDOW = 512
FORCE_SCORE = 1e4
N_BUCKETS = 32
MAX_DIST = 1024
N_MEM = 256
XA_HEADS = 4
XA_HEAD_DIM = 128
XA_W = 512
D_FF = 5632
ALPHA = (2.0 * DEPTH) ** 0.25
LN_EPS = 1e-5
NEG = -1e30

LANE = 128
SUBLANE = 8
MXU_DIM = 256
VMEM_LIMIT = 56 * 1024 * 1024
TQ = 512
TK = 256
SUB_Q = TQ // LANE
SUB_K = TK // LANE
HEADS_PER_STEP = 4
EXP_ROWS = 32
S5_SEG = 64
S5_CHUNK = SUBLANE * S5_SEG
S5_CB = 8
BIAS_SUB = LANE
FAR_SUB = MAX_DIST // BIAS_SUB + 1
N_BIAS_SUB = FAR_SUB + SUB_K + SUB_Q - 2

HT_SBQ = 0
HT_NQ = HT_SBQ + SB_HEADS
HT_Q_ROWS = (HT_NQ + NSA_HEADS) * HEAD_DIM
HT_SBV = 0
HT_VSL = HT_SBV + SB_HEADS
HT_VSW = HT_VSL + NSA_KV
HT_V_ROWS = (HT_VSW + NSA_KV) * HEAD_DIM
HN_SBK = 0
HN_KC = HN_SBK + SB_HEADS
HN_KSL = HN_KC + 2 * NSA_KV
HN_KSW = HN_KSL + NSA_KV
HN_HEADS = HN_KSW + NSA_KV


def _cparams(sem):
    return pltpu.CompilerParams(dimension_semantics=sem, vmem_limit_bytes=VMEM_LIMIT)


def _dot(a, b):
    return jnp.dot(a, b, preferred_element_type=F32)


def _dot_nt(a, b):
    return lax.dot_general(a, b, (((1,), (1,)), ((), ())), preferred_element_type=F32)


def _split_bf16(x):
    hi = x.astype(BF16)
    lo = (x - hi.astype(F32)).astype(BF16)
    return hi, lo


def _gelu(x):
    c = math.sqrt(2.0 / math.pi)
    return 0.5 * x * (1.0 + jnp.tanh(c * (x + 0.044715 * (x * x * x))))


def _sigmoid(x):
    return 1.0 / (1.0 + jnp.exp(-x))


def _layer_norm(x, g, b):
    mu = jnp.mean(x, axis=-1, keepdims=True)
    xc = x - mu
    var = jnp.mean(xc * xc, axis=-1, keepdims=True)
    return xc * lax.rsqrt(var + LN_EPS) * g + b


def _mm_kernel(a_ref, b_ref, o_ref):
    o_ref[...] = _dot(a_ref[...], b_ref[...]).astype(o_ref.dtype)


def _mm(a, b, out_dtype, tm, tn):
    m, k = a.shape
    n = b.shape[1]
    tm = min(tm, m)
    return pl.pallas_call(
        _mm_kernel,
        grid=(n // tn, m // tm),
        in_specs=[pl.BlockSpec((tm, k), lambda j, i: (i, 0)),
                  pl.BlockSpec((k, tn), lambda j, i: (0, j))],
        out_specs=pl.BlockSpec((tm, tn), lambda j, i: (i, j)),
        out_shape=jax.ShapeDtypeStruct((m, n), out_dtype),
        compiler_params=_cparams(("parallel", "arbitrary")),
        name="mm",
    )(a, b)


def _mm_heads_kernel(a_ref, b_ref, o_ref):
    o = _dot(a_ref[...], b_ref[...]).astype(o_ref.dtype)
    for r in range(o_ref.shape[0]):
        o_ref[r] = o[:, r * HEAD_DIM:(r + 1) * HEAD_DIM]


def _mm_heads(a, b, tm):
    m, k = a.shape
    n = b.shape[1]
    tn = MXU_DIM
    hpt = tn // HEAD_DIM
    tm = min(tm, m)
    return pl.pallas_call(
        _mm_heads_kernel,
        grid=(n // tn, m // tm),
        in_specs=[pl.BlockSpec((tm, k), lambda j, i: (i, 0)),
                  pl.BlockSpec((k, tn), lambda j, i: (0, j))],
        out_specs=pl.BlockSpec((hpt, tm, HEAD_DIM), lambda j, i: (j, i, 0)),
        out_shape=jax.ShapeDtypeStruct((n // HEAD_DIM, m, HEAD_DIM), BF16),
        compiler_params=_cparams(("parallel", "arbitrary")),
        name="mm_heads",
    )(a, b)


def _mm_nt_kernel(w_ref, a_ref, o_ref):
    o_ref[0] = _dot_nt(w_ref[...], a_ref[...]).astype(o_ref.dtype)


def _mm_nt(w_t, a, out_dtype, tn, tm):
    n, k = w_t.shape
    m = a.shape[0]
    return pl.pallas_call(
        _mm_nt_kernel,
        grid=(n // tn, m // tm),
        in_specs=[pl.BlockSpec((tn, k), lambda j, i: (j, 0)),
                  pl.BlockSpec((tm, k), lambda j, i: (i, 0))],
        out_specs=pl.BlockSpec((1, tn, tm), lambda j, i: (i, j, 0)),
        out_shape=jax.ShapeDtypeStruct((m // tm, n, tm), out_dtype),
        compiler_params=_cparams(("parallel", "arbitrary")),
        name="mm_nt",
    )(w_t, a)


def _mm_ln_kernel(a_ref, w_ref, h_ref, g_ref, b_ref, of_ref, ob_ref, acc_ref):
    kk = pl.program_id(1)

    @pl.when(kk == 0)
    def _():
        acc_ref[...] = jnp.zeros_like(acc_ref)

    acc_ref[...] += _dot(a_ref[...], w_ref[...])

    @pl.when(kk == pl.num_programs(1) - 1)
    def _():
        y = _layer_norm(ALPHA * h_ref[...] + acc_ref[...], g_ref[...], b_ref[...])
        of_ref[...] = y
        ob_ref[...] = y.astype(BF16)


def _mm_ln(a, w, h, g, b, tm, tk):
    m, k = a.shape
    n = w.shape[1]
    tm = min(tm, m)
    return pl.pallas_call(
        _mm_ln_kernel,
        grid=(m // tm, k // tk),
        in_specs=[pl.BlockSpec((tm, tk), lambda i, kk: (i, kk)),
                  pl.BlockSpec((tk, n), lambda i, kk: (kk, 0)),
                  pl.BlockSpec((tm, n), lambda i, kk: (i, 0)),
                  pl.BlockSpec((1, n), lambda i, kk: (0, 0)),
                  pl.BlockSpec((1, n), lambda i, kk: (0, 0))],
        out_specs=[pl.BlockSpec((tm, n), lambda i, kk: (i, 0)),
                   pl.BlockSpec((tm, n), lambda i, kk: (i, 0))],
        out_shape=[jax.ShapeDtypeStruct((m, n), F32), jax.ShapeDtypeStruct((m, n), BF16)],
        scratch_shapes=[pltpu.VMEM((tm, n), F32)],
        compiler_params=_cparams(("parallel", "arbitrary")),
        name="mm_ln",
    )(a, w, h, g.reshape(1, n), b.reshape(1, n))


def _s5_kernel(u_ref, bm_ref, cm_ref, a_ref, d_ref, z_ref, s_ref, p_ref, carry_ref):
    tc = pl.program_id(1)
    w = s_ref.shape[1] // 2
    n_step = p_ref.shape[0]
    ar1 = a_ref[0:1, :]
    ai1 = a_ref[1:2, :]

    @pl.when(tc == 0)
    def _():
        carry_ref[...] = jnp.zeros_like(carry_ref)
        p_ref[0:1, 0:w] = ar1
        p_ref[0:1, w:2 * w] = ai1

        def pw(t, c):
            pr, pi = c
            nr = pr * ar1 - pi * ai1
            ni = pr * ai1 + pi * ar1
            p_ref[pl.ds(t, 1), 0:w] = nr
            p_ref[pl.ds(t, 1), w:2 * w] = ni
            return nr, ni

        lax.fori_loop(1, n_step, pw, (ar1, ai1))

    u = u_ref[...]
    s_ref[...] = _dot(u.astype(BF16), bm_ref[0])

    ar = jnp.broadcast_to(ar1, (SUBLANE, w))
    ai = jnp.broadcast_to(ai1, (SUBLANE, w))

    def step(t, c):
        sr, si = c
        r0 = pl.multiple_of(t * SUBLANE, SUBLANE)
        xr = s_ref[pl.ds(r0, SUBLANE), 0:w]
        xi = s_ref[pl.ds(r0, SUBLANE), w:2 * w]
        nr = ar * sr - ai * si + xr
        ni = ar * si + ai * sr + xi
        s_ref[pl.ds(r0, SUBLANE), 0:w] = nr
        s_ref[pl.ds(r0, SUBLANE), w:2 * w] = ni
        return nr, ni

    zero = jnp.zeros((SUBLANE, w), F32)
    fr, fi = lax.fori_loop(0, n_step, step, (zero, zero), unroll=8)

    amr = p_ref[n_step - 1:n_step, 0:w]
    ami = p_ref[n_step - 1:n_step, w:2 * w]
    row = lax.broadcasted_iota(jnp.int32, (SUBLANE, w), 0)
    cr = carry_ref[0:1, 0:w]
    ci = carry_ref[0:1, w:2 * w]
    cin_r = jnp.zeros((SUBLANE, w), F32)
    cin_i = jnp.zeros((SUBLANE, w), F32)
    for seg in range(SUBLANE):
        cin_r = jnp.where(row == seg, cr, cin_r)
        cin_i = jnp.where(row == seg, ci, cin_i)
        nr = fr[seg:seg + 1, :] + amr * cr - ami * ci
        ni = fi[seg:seg + 1, :] + amr * ci + ami * cr
        cr, ci = nr, ni
    carry_ref[0:1, 0:w] = cr
    carry_ref[0:1, w:2 * w] = ci

    def fix(t, c):
        r0 = pl.multiple_of(t * SUBLANE, SUBLANE)
        pr = p_ref[pl.ds(t, 1), 0:w]
        pi = p_ref[pl.ds(t, 1), w:2 * w]
        s_ref[pl.ds(r0, SUBLANE), 0:w] += pr * cin_r - pi * cin_i
        s_ref[pl.ds(r0, SUBLANE), w:2 * w] += pr * cin_i + pi * cin_r
        return c

    lax.fori_loop(0, n_step, fix, 0, unroll=8)

    y = _dot(s_ref[...].astype(BF16), cm_ref[0]) + d_ref[...] * u
    z_ref[...] = _gelu(y)


def _s5_glu_kernel(z_ref, w_ref, b_ref, o_ref):
    z = z_ref[...]
    gate = _sigmoid(_dot(z.astype(BF16), w_ref[...]) + b_ref[...])
    o_ref[...] = (z * gate).astype(o_ref.dtype)


def _s5_params(lam_re, lam_im, log_dt, b_re, b_im, c_re, c_im):
    delta = jnp.exp(log_dt)[:, None]
    mag = jnp.exp(lam_re * delta)
    ar = mag * jnp.cos(lam_im * delta)
    ai = mag * jnp.sin(lam_im * delta)
    den = lam_re * lam_re + lam_im * lam_im
    cr = ((ar - 1.0) * lam_re + ai * lam_im) / den
    ci = (ai * lam_re - (ar - 1.0) * lam_im) / den
    br = cr[..., None] * b_re - ci[..., None] * b_im
    bi = cr[..., None] * b_im + ci[..., None] * b_re
    ncb = S5_NG // S5_CB
    eye = jnp.eye(S5_CB, dtype=F32)

    def in_mat(b):
        b = b.reshape(ncb, S5_CB, S5_P, S5_GROUP)
        m = jnp.einsum('ngpc,gh->ngchp', b, eye)
        return m.reshape(ncb, S5_CB * S5_GROUP, S5_CB * S5_P)

    def out_mat(c):
        c = c.reshape(ncb, S5_CB, S5_GROUP, S5_P)
        m = jnp.einsum('ngcp,gh->ngphc', c, eye)
        return m.reshape(ncb, S5_CB * S5_P, S5_CB * S5_GROUP)

    bm = jnp.concatenate([in_mat(br), in_mat(bi)], axis=2).astype(BF16)
    cm = jnp.concatenate([out_mat(c_re), -out_mat(c_im)], axis=1).astype(BF16)
    a = jnp.stack([ar.reshape(-1), ai.reshape(-1)], axis=0)
    return bm, cm, a


def _s5(u, p, w_glu, b_glu, d):
    t_len = u.shape[0]
    nc = t_len // S5_CHUNK
    ncb = S5_NG // S5_CB
    wl = S5_CB * S5_GROUP
    ws = S5_CB * S5_P
    bm, cm, a = p
    up = u.reshape(nc, SUBLANE, S5_SEG, S5_CH).transpose(0, 2, 1, 3).reshape(t_len, S5_CH)
    z = pl.pallas_call(
        _s5_kernel,
        grid=(ncb, nc),
        in_specs=[pl.BlockSpec((S5_CHUNK, wl), lambda c, t: (t, c)),
                  pl.BlockSpec((1, wl, 2 * ws), lambda c, t: (c, 0, 0)),
                  pl.BlockSpec((1, 2 * ws, wl), lambda c, t: (c, 0, 0)),
                  pl.BlockSpec((2, ws), lambda c, t: (0, c)),
                  pl.BlockSpec((1, wl), lambda c, t: (0, c))],
        out_specs=pl.BlockSpec((S5_CHUNK, wl), lambda c, t: (t, c)),
        out_shape=jax.ShapeDtypeStruct((t_len, S5_CH), F32),
        scratch_shapes=[pltpu.VMEM((S5_CHUNK, 2 * ws), F32),
                        pltpu.VMEM((S5_SEG, 2 * ws), F32),
                        pltpu.VMEM((SUBLANE, 2 * ws), F32)],
        compiler_params=_cparams(("parallel", "arbitrary")),
        name="s5_scan",
    )(up, bm, cm, a, d.reshape(1, S5_CH))
    tm = min(1024, t_len)
    y = pl.pallas_call(
        _s5_glu_kernel,
        grid=(t_len // tm,),
        in_specs=[pl.BlockSpec((tm, S5_CH), lambda i: (i, 0)),
                  pl.BlockSpec((S5_CH, S5_CH), lambda i: (0, 0)),
                  pl.BlockSpec((1, S5_CH), lambda i: (0, 0))],
        out_specs=pl.BlockSpec((tm, S5_CH), lambda i: (i, 0)),
        out_shape=jax.ShapeDtypeStruct((t_len, S5_CH), BF16),
        compiler_params=_cparams(("parallel",)),
        name="s5_glu",
    )(z, w_glu.astype(BF16), b_glu.reshape(1, S5_CH))
    return y.reshape(nc, S5_SEG, SUBLANE, S5_CH).transpose(0, 2, 1, 3).reshape(t_len, S5_CH)


def _sb_kernel(q_ref, k_ref, v_ref, u_ref, o_ref, r_ref, acc_ref, nl_ref, arg_ref, w_ref):
    qi = pl.program_id(1)
    ucat = u_ref[...]
    qs = [q_ref[0, r * HEAD_DIM:(r + 1) * HEAD_DIM, :] * 0.125 for r in range(HEADS_PER_STEP)]
    r_ref[...] = jnp.zeros_like(r_ref)
    acc_ref[...] = jnp.zeros_like(acc_ref)
    key = lax.broadcasted_iota(jnp.int32, (TK, TQ), 0)
    qry = lax.broadcasted_iota(jnp.int32, (TK, TQ), 1)

    def tile(kt, diag_off):
        k0 = pl.multiple_of(kt * TK, TK)
        mask = None if diag_off is None else (key + diag_off) < qry
        sums = []
        for r in range(HEADS_PER_STEP):
            z = _dot(k_ref[r, pl.ds(k0, TK), :], qs[r])
            nl = jnp.maximum(z, 0.0) + jnp.log(1.0 + jnp.exp(-jnp.abs(z)))
            arg_ref[r] = z - nl
            if mask is not None:
                nl = jnp.where(mask, nl, 0.0)
            hi, lo = _split_bf16(nl)
            nl_ref[r, 0:TK, :] = hi
            nl_ref[r, TK:2 * TK, :] = lo
            sums.append(jnp.sum(nl, axis=0, keepdims=True))
        for r in range(HEADS_PER_STEP):
            later = _dot(ucat, nl_ref[r])
            w = jnp.exp(arg_ref[r] - later)
            if mask is not None:
                w = jnp.where(mask, w, 0.0)
            w_ref[r] = w.astype(BF16)
        for r in range(HEADS_PER_STEP):
            v = v_ref[kt, r * HEAD_DIM:(r + 1) * HEAD_DIM, :]
            r_old = r_ref[r, 0:1, :]
            acc_ref[r] += _dot(v, w_ref[r]) * jnp.exp(-r_old)
            r_ref[r, 0:1, :] = r_old + sums[r]

    n_diag = TQ // TK
    for c in reversed(range(n_diag)):
        tile(n_diag * qi + c, c * TK)

    def alive():
        return (jnp.max(jnp.exp(-r_ref[:, 0:1, :])) > 0.0).astype(jnp.int32)

    def cond(state):
        return (state[0] >= 0) & (state[1] > 0)

    def body(state):
        tile(state[0], None)
        return state[0] - 1, alive()

    lax.while_loop(cond, body, (n_diag * qi - 1, alive()))
    o_t = jnp.concatenate([acc_ref[r] for r in range(HEADS_PER_STEP)], axis=0)
    o_ref[...] = o_t.T.astype(o_ref.dtype)


def _stick_breaking(proj, q_t, v_t, ucat):
    t_len = proj.shape[1]
    nt = t_len // TQ
    hw = HEADS_PER_STEP * HEAD_DIM
    return pl.pallas_call(
        _sb_kernel,
        grid=(SB_HEADS // HEADS_PER_STEP, nt),
        in_specs=[pl.BlockSpec((1, hw, TQ), lambda h, i: (i, HT_SBQ // HEADS_PER_STEP + h, 0)),
                  pl.BlockSpec((HEADS_PER_STEP, t_len, HEAD_DIM),
                               lambda h, i: (HN_SBK // HEADS_PER_STEP + h, 0, 0)),
                  pl.BlockSpec((t_len // TK, hw, TK),
                               lambda h, i: (0, HT_SBV // HEADS_PER_STEP + h, 0)),
                  pl.BlockSpec((TK, 2 * TK), lambda h, i: (0, 0))],
        out_specs=pl.BlockSpec((TQ, hw), lambda h, i: (i, h)),
        out_shape=jax.ShapeDtypeStruct((t_len, SB_HEADS * HEAD_DIM), BF16),
        scratch_shapes=[pltpu.VMEM((HEADS_PER_STEP, SUBLANE, TQ), F32),
                        pltpu.VMEM((HEADS_PER_STEP, HEAD_DIM, TQ), F32),
                        pltpu.VMEM((HEADS_PER_STEP, 2 * TK, TQ), BF16),
                        pltpu.VMEM((HEADS_PER_STEP, TK, TQ), F32),
                        pltpu.VMEM((HEADS_PER_STEP, TK, TQ), BF16)],
        compiler_params=_cparams(("parallel", "arbitrary")),
        name="stick_breaking",
    )(q_t, proj, v_t, ucat)


def _t5_bucket(dist):
    n = jnp.maximum(dist, 0)
    max_exact = N_BUCKETS // 2
    nf = jnp.maximum(n, 1).astype(jnp.float32)
    large = max_exact + (jnp.log(nf / max_exact) / math.log(MAX_DIST / max_exact)
                         * (N_BUCKETS - max_exact)).astype(jnp.int32)
    large = jnp.minimum(large, N_BUCKETS - 1)
    return jnp.where(n < max_exact, n, large)


def _bias_lookup_kernel(tab_ref, idx_ref, o_ref):
    h = pl.program_id(0)
    idx = idx_ref[...]
    acc = jnp.zeros(idx.shape, F32)
    for b in range(N_BUCKETS):
        acc = jnp.where(idx == b, tab_ref[b, h], acc)
    o_ref[0] = acc


def _bias_lookup(tab, idx):
    r, c = idx.shape
    n_h = tab.shape[1]
    return pl.pallas_call(
        _bias_lookup_kernel,
        grid=(n_h,),
        in_specs=[pl.BlockSpec(memory_space=pltpu.SMEM),
                  pl.BlockSpec((r, c), lambda h: (0, 0))],
        out_specs=pl.BlockSpec((1, r, c), lambda h: (h, 0, 0)),
        out_shape=jax.ShapeDtypeStruct((n_h, r, c), F32),
        compiler_params=_cparams(("arbitrary",)),
        name="bias_lookup",
    )(tab, idx)


def _bias_tables(rel_bias):
    i = jnp.arange(BIAS_SUB)
    d_win = (BIAS_SUB * jnp.arange(N_BIAS_SUB)[:, None, None] + i[None, None, :] - i[None, :, None])
    idx_win = _t5_bucket(d_win).reshape(N_BIAS_SUB * BIAS_SUB, BIAS_SUB)
    win = _bias_lookup(rel_bias, idx_win).reshape(NSA_HEADS, N_BIAS_SUB, BIAS_SUB, BIAS_SUB)
    n_m = (LANE * CMP_STRIDE) // TQ
    m = jnp.arange(n_m)[:, None, None]
    c = jnp.arange(2 * LANE)[None, :, None]
    qi = jnp.arange(TQ)[None, None, :]
    d_cmp = TQ * m + qi + LANE * CMP_STRIDE - CMP_STRIDE * c - (CMP_LEN - 1)
    idx_cmp = _t5_bucket(d_cmp).reshape(n_m * 2 * LANE, TQ)
    cmp_t = _bias_lookup(rel_bias, idx_cmp).reshape(NSA_HEADS, n_m, 2 * LANE, TQ)
    return win, cmp_t


def _compress_kernel(c0_ref, c1_ref, w1_ref, pos_ref, w2_ref, w2t_ref, o_ref, ot_ref):
    half = c0_ref.shape[3]
    w1 = w1_ref[0]
    posb = _dot(pos_ref[0], w1)[0:1, :]
    hid = _dot(c0_ref[0, 0], w1[0:half, :]) + _dot(c1_ref[0, 0], w1[half:, :]) + posb
    act = _gelu(hid).astype(BF16)
    o_ref[0, 0] = _dot(act, w2_ref[0]).astype(o_ref.dtype)
    ot_ref[0, 0] = _dot_nt(w2t_ref[0], act).astype(ot_ref.dtype)


def _compress(proj, head0, cmp_pos, cmp_w1, cmp_w2):
    t_len = proj.shape[1]
    n16 = t_len // CMP_STRIDE
    half = CMP_STRIDE * HEAD_DIM
    kv = proj[head0:head0 + 2 * NSA_KV].reshape(2, NSA_KV, n16, half)
    kv_next = jnp.concatenate([kv[:, :, 1:], jnp.zeros((2, NSA_KV, 1, half), BF16)], axis=2)
    pos = jnp.broadcast_to(cmp_pos.reshape(2, 1, CMP_LEN * HEAD_DIM), (2, SUBLANE, CMP_LEN * HEAD_DIM))
    w2 = cmp_w2.astype(BF16)
    return pl.pallas_call(
        _compress_kernel,
        grid=(2, NSA_KV),
        in_specs=[pl.BlockSpec((1, 1, n16, half), lambda j, g: (j, g, 0, 0)),
                  pl.BlockSpec((1, 1, n16, half), lambda j, g: (j, g, 0, 0)),
                  pl.BlockSpec((1, 2 * half, CMP_HID), lambda j, g: (j, 0, 0)),
                  pl.BlockSpec((1, SUBLANE, 2 * half), lambda j, g: (j, 0, 0)),
                  pl.BlockSpec((1, CMP_HID, HEAD_DIM), lambda j, g: (j, 0, 0)),
                  pl.BlockSpec((1, HEAD_DIM, CMP_HID), lambda j, g: (j, 0, 0))],
        out_specs=[pl.BlockSpec((1, 1, n16, HEAD_DIM), lambda j, g: (j, g, 0, 0)),
                   pl.BlockSpec((1, 1, HEAD_DIM, n16), lambda j, g: (j, g, 0, 0))],
        out_shape=[jax.ShapeDtypeStruct((2, NSA_KV, n16, HEAD_DIM), BF16),
                   jax.ShapeDtypeStruct((2, NSA_KV, HEAD_DIM, n16), BF16)],
        compiler_params=_cparams(("parallel", "arbitrary")),
        name="nsa_compress",
    )(kv, kv_next, cmp_w1.astype(BF16), pos.astype(BF16), w2, w2.transpose(0, 2, 1))


def _cmp_select_kernel(q_ref, kc_ref, vct_ref, bias_ref, tab_ref, ov_ref, oc_ref, mn_ref, *, n_cmp):
    g = pl.program_id(0)
    qi = pl.program_id(1)
    n16 = kc_ref.shape[2]
    n_ct = n16 // LANE
    kc = kc_ref[0, 0]
    vct = vct_ref[0, 0]
    q_per_ct = (LANE * CMP_STRIDE) // TQ
    jd = qi // q_per_ct
    n = lax.broadcasted_iota(jnp.int32, (n16, TQ), 0)
    t = qi * TQ + lax.broadcasted_iota(jnp.int32, (n16, TQ), 1)
    valid = (CMP_STRIDE * n + (CMP_LEN - 1) <= t) & (n < n_cmp)
    psum = jnp.zeros((n16, TQ), F32)
    for r in range(NSA_REP):
        q = q_ref[0, r * HEAD_DIM:(r + 1) * HEAD_DIM, :] * 0.125
        far = tab_ref[N_BUCKETS - 1, g * NSA_REP + r]
        rows = []
        for jc in range(n_ct):
            near = jnp.where(jc == jd, bias_ref[r, 0, LANE:2 * LANE, :], bias_ref[r, 0, 0:LANE, :])
            rows.append(jnp.where((jc == jd) | (jc == jd - 1), near, far))
        bias = rows[0] if n_ct == 1 else jnp.concatenate(rows, axis=0)
        s = jnp.where(valid, _dot(kc, q) + bias, NEG)
        mx = jnp.max(s, axis=0, keepdims=True)
        e = jnp.where(valid, jnp.exp(s - mx), 0.0)
        p = e * (1.0 / jnp.maximum(jnp.sum(e, axis=0, keepdims=True), 1e-30))
        psum = psum + p
        oc_ref[r * HEAD_DIM:(r + 1) * HEAD_DIM, :] = _dot(vct, p.astype(BF16))
    hi, lo = _split_bf16(psum)
    imp = _dot(ov_ref[...], jnp.concatenate([hi, lo], axis=0))
    n_sel = imp.shape[0]
    j = lax.broadcasted_iota(jnp.int32, (n_sel, LANE), 0)
    jf = j.astype(F32)
    for cc in range(TQ // LANE):
        tq = qi * TQ + cc * LANE + lax.broadcasted_iota(jnp.int32, (n_sel, LANE), 1)
        cur = tq // SEL_BLOCK
        forced = (j == 0) | (j == cur) | (j == cur - 1)
        score = jnp.where(j <= cur,
                          jnp.where(forced, FORCE_SCORE, imp[:, cc * LANE:(cc + 1) * LANE]), -1.0)
        sel = jnp.zeros((n_sel, LANE), F32)
        for _ in range(min(SEL_TOPN, n_sel)):
            mx = jnp.max(score, axis=0, keepdims=True)
            first = jnp.min(jnp.where(score == mx, jf, float(n_sel)), axis=0, keepdims=True)
            hit = jf == first
            sel = jnp.where(hit, 1.0, sel)
            score = jnp.where(hit, -jnp.inf, score)
        mn_ref[0, :, cc * LANE:(cc + 1) * LANE] = (sel - 1.0).astype(mn_ref.dtype)


def _cmp_select(q_t, kc, vct, bias_cmp, rel_bias, overlap_cat):
    nt = q_t.shape[0]
    t_len = nt * TQ
    n16 = kc.shape[2]
    n_sel = t_len // SEL_BLOCK
    n_cmp = (t_len - CMP_LEN) // CMP_STRIDE + 1
    q_per_ct = (LANE * CMP_STRIDE) // TQ
    hw = NSA_REP * HEAD_DIM
    return pl.pallas_call(
        functools.partial(_cmp_select_kernel, n_cmp=n_cmp),
        grid=(NSA_KV, nt),
        in_specs=[pl.BlockSpec((1, hw, TQ), lambda g, i: (i, HT_NQ // NSA_REP + g, 0)),
                  pl.BlockSpec((1, 1, n16, HEAD_DIM), lambda g, i: (0, g, 0, 0)),
                  pl.BlockSpec((1, 1, HEAD_DIM, n16), lambda g, i: (1, g, 0, 0)),
                  pl.BlockSpec((NSA_REP, 1, 2 * LANE, TQ), lambda g, i: (g, i % q_per_ct, 0, 0)),
                  pl.BlockSpec(memory_space=pltpu.SMEM),
                  pl.BlockSpec((n_sel, 2 * n16), lambda g, i: (0, 0))],
        out_specs=[pl.BlockSpec((hw, TQ), lambda g, i: (g, i)),
                   pl.BlockSpec((1, n_sel, TQ), lambda g, i: (g, 0, i))],
        out_shape=[jax.ShapeDtypeStruct((NSA_HEADS * HEAD_DIM, t_len), F32),
                   jax.ShapeDtypeStruct((NSA_KV, n_sel, t_len), BF16)],
        compiler_params=_cparams(("parallel", "arbitrary")),
        name="nsa_cmp_select",
    )(q_t, kc, vct, bias_cmp, rel_bias, overlap_cat)


def _bias_tile(win_ref, r, base):
    rows = []
    for b in range(SUB_K):
        cols = []
        for a in range(SUB_Q):
            idx = jnp.maximum(base + a - b, 0)
            cols.append(win_ref[r, idx])
        rows.append(jnp.concatenate(cols, axis=1))
    return jnp.concatenate(rows, axis=0)


def _slc_swa_kernel(q_ref, mn_ref, ka_ref, vs_ref, kw_ref, vw_ref, win_ref, tab_ref, gate_ref,
                    oc_ref, o_ref, qa_ref, m_ref, l_ref, acc_ref, s_ref, p_ref):
    g = pl.program_id(0)
    qi = pl.program_id(1)
    n_sel = mn_ref.shape[1]
    n_diag = TQ // TK
    big = mn_ref[0] * jnp.asarray(1e30, BF16)
    for r in range(NSA_REP):
        qa_ref[r, 0:HEAD_DIM, :] = q_ref[0, r * HEAD_DIM:(r + 1) * HEAD_DIM, :] * 0.125
        qa_ref[r, HEAD_DIM:LANE, :] = jnp.zeros((LANE - HEAD_DIM, TQ), BF16)
        qa_ref[r, LANE:LANE + n_sel, :] = big
        if n_sel < LANE:
            qa_ref[r, LANE + n_sel:2 * LANE, :] = jnp.zeros((LANE - n_sel, TQ), BF16)
    m_ref[...] = jnp.full_like(m_ref, NEG)
    l_ref[...] = jnp.zeros_like(l_ref)
    acc_ref[...] = jnp.zeros_like(acc_ref)
    key = lax.broadcasted_iota(jnp.int32, (TK, TQ), 0)
    qry = lax.broadcasted_iota(jnp.int32, (TK, TQ), 1)

    def scores(kt, far, diag_off, buf):
        k0 = pl.multiple_of(kt * TK, TK)
        ka = ka_ref[0, pl.ds(k0, TK), :]
        for r in range(NSA_REP):
            s = _dot(ka, qa_ref[r])
            if not far:
                s = s + _bias_tile(win_ref, r, SUB_Q * qi - SUB_K * kt)
            if diag_off is not None:
                s = jnp.where(key + diag_off <= qry, s, NEG)
            s_ref[buf, r] = s

    def consume(kt, buf):
        v = vs_ref[kt]
        stats = []
        groups = EXP_ROWS // SUBLANE

        def rows8(c0):
            return s_ref[buf, r, c0:c0 + EXP_ROWS, :].reshape(groups, SUBLANE, TQ)

        for r in range(NSA_REP):
            mx8 = jnp.max(rows8(0), axis=0)
            for c0 in range(EXP_ROWS, TK, EXP_ROWS):
                mx8 = jnp.maximum(mx8, jnp.max(rows8(c0), axis=0))
            m_old = m_ref[r, 0:1, :]
            m_new = jnp.maximum(m_old, jnp.max(mx8, axis=0, keepdims=True))
            m_ref[r, 0:1, :] = m_new
            stats.append((m_new, jnp.exp(m_old - m_new)))
        for r in range(NSA_REP):
            m_new, alpha = stats[r]
            ps8 = jnp.zeros((SUBLANE, TQ), F32)
            for c0 in range(0, TK, EXP_ROWS):
                p = jnp.exp(s_ref[buf, r, c0:c0 + EXP_ROWS, :] - m_new)
                ps8 = ps8 + jnp.sum(p.reshape(groups, SUBLANE, TQ), axis=0)
                p_ref[r, c0:c0 + EXP_ROWS, :] = p.astype(BF16)
            l_ref[r, 0:1, :] = alpha * l_ref[r, 0:1, :] + jnp.sum(ps8, axis=0, keepdims=True)
        for r in range(NSA_REP):
            acc_ref[r] = stats[r][1] * acc_ref[r] + _dot(v, p_ref[r])

    def slc_tile(kt, far, diag_off):
        scores(kt, far, diag_off, 0)
        consume(kt, 0)

    n_far = jnp.maximum((SUB_Q * qi - (SUB_K - 1) - FAR_SUB) // SUB_K + 1, 0)

    scores(0, True, None, 0)

    def far_body(j, c):
        scores(2 * j + 1, True, None, 1)
        consume(2 * j, 0)
        scores(2 * j + 2, True, None, 0)
        consume(2 * j + 1, 1)
        return c

    lax.fori_loop(0, n_far // 2, far_body, 0)

    @pl.when(n_far % 2 == 1)
    def _():
        consume(n_far - 1, 0)
    for r in range(NSA_REP):
        m_ref[r, 0:1, :] = m_ref[r, 0:1, :] + tab_ref[N_BUCKETS - 1, g * NSA_REP + r]

    n_band = n_diag * qi - n_far
    scores(n_far, False, None, 0)

    def band_body(j, c):
        kt = n_far + 2 * j
        scores(kt + 1, False, None, 1)
        consume(kt, 0)
        scores(kt + 2, False, None, 0)
        consume(kt + 1, 1)
        return c

    lax.fori_loop(0, n_band // 2, band_body, 0)

    @pl.when(n_band % 2 == 1)
    def _():
        consume(n_diag * qi - 1, 0)
    for c in range(n_diag):
        slc_tile(n_diag * qi + c, False, c * TK)

    n_wt = WINDOW // TK + n_diag
    outs = []
    for r in range(NSA_REP):
        q = qa_ref[r, 0:HEAD_DIM, :]
        ss, vv, oks = [], [], []
        for c in range(n_wt):
            back = WINDOW // TK - c
            kt = n_diag * qi - back
            ktc = jnp.maximum(kt, 0)
            k0 = pl.multiple_of(ktc * TK, TK)
            s = _dot(kw_ref[0, pl.ds(k0, TK), :], q) + _bias_tile(win_ref, r, SUB_K * back)
            dist = back * TK + qry - key
            ok = (dist >= 0) & (dist < WINDOW) & (kt >= 0)
            ss.append(jnp.where(ok, s, NEG))
            oks.append(ok)
            vv.append(vw_ref[ktc])
        mx = jnp.max(ss[0], axis=0, keepdims=True)
        for s in ss[1:]:
            mx = jnp.maximum(mx, jnp.max(s, axis=0, keepdims=True))
        den = jnp.zeros((1, TQ), F32)
        o_w = jnp.zeros((HEAD_DIM, TQ), F32)
        for s, v, ok in zip(ss, vv, oks):
            e = jnp.where(ok, jnp.exp(s - mx), 0.0)
            den = den + jnp.sum(e, axis=0, keepdims=True)
            o_w = o_w + _dot(v, e.astype(BF16))
        o_w = o_w * (1.0 / jnp.maximum(den, 1e-30))
        o_s = acc_ref[r] * (1.0 / jnp.maximum(l_ref[r, 0:1, :], 1e-30))
        o_c = oc_ref[r * HEAD_DIM:(r + 1) * HEAD_DIM, :]
        row0 = 3 * (g * NSA_REP + r)
        gates = [_sigmoid(gate_ref[0, pl.ds(row0 + c, 1), :]) for c in range(3)]
        outs.append(gates[0] * o_c + gates[1] * o_s + gates[2] * o_w)
    o_ref[...] = jnp.concatenate(outs, axis=0).T.astype(o_ref.dtype)


def _gate_weights_t(w_gate):
    pad = HEAD_DIM - w_gate.shape[1]
    return jnp.pad(w_gate.T, ((0, pad), (0, 0))).astype(BF16)


def _key_augment(proj, onehot):
    t_len = proj.shape[1]
    n_sel = t_len // SEL_BLOCK
    pad = LANE - n_sel
    return jnp.concatenate(
        [proj[HN_KSL:HN_KSL + NSA_KV],
         jnp.zeros((NSA_KV, t_len, LANE - HEAD_DIM), BF16),
         jnp.broadcast_to(jnp.pad(onehot, ((0, 0), (0, pad)))[None], (NSA_KV, t_len, LANE))], axis=2)


def _slc_swa(proj, q_t, v_t, mask_t, k_aug, bias_win, rel_bias, gate_t, o_cmp_t):
    t_len = proj.shape[1]
    nt = t_len // TQ
    nkt = t_len // TK
    n_sel = t_len // SEL_BLOCK
    hw = NSA_REP * HEAD_DIM
    return pl.pallas_call(
        _slc_swa_kernel,
        grid=(NSA_KV, nt),
        in_specs=[pl.BlockSpec((1, hw, TQ), lambda g, i: (i, HT_NQ // NSA_REP + g, 0)),
                  pl.BlockSpec((1, n_sel, TQ), lambda g, i: (g, 0, i)),
                  pl.BlockSpec((1, t_len, 2 * LANE), lambda g, i: (g, 0, 0)),
                  pl.BlockSpec((nkt, HEAD_DIM, TK), lambda g, i: (0, HT_VSL + g, 0)),
                  pl.BlockSpec((1, t_len, HEAD_DIM), lambda g, i: (HN_KSW + g, 0, 0)),
                  pl.BlockSpec((nkt, HEAD_DIM, TK), lambda g, i: (0, HT_VSW + g, 0)),
                  pl.BlockSpec((NSA_REP, N_BIAS_SUB, BIAS_SUB, BIAS_SUB), lambda g, i: (g, 0, 0, 0)),
                  pl.BlockSpec(memory_space=pltpu.SMEM),
                  pl.BlockSpec((1, HEAD_DIM, TQ), lambda g, i: (i, 0, 0)),
                  pl.BlockSpec((hw, TQ), lambda g, i: (g, i))],
        out_specs=pl.BlockSpec((TQ, hw), lambda g, i: (i, g)),
        out_shape=jax.ShapeDtypeStruct((t_len, NSA_HEADS * HEAD_DIM), BF16),
        scratch_shapes=[pltpu.VMEM((NSA_REP, 2 * LANE, TQ), BF16),
                        pltpu.VMEM((NSA_REP, SUBLANE, TQ), F32),
                        pltpu.VMEM((NSA_REP, SUBLANE, TQ), F32),
                        pltpu.VMEM((NSA_REP, HEAD_DIM, TQ), F32),
                        pltpu.VMEM((2, NSA_REP, TK, TQ), F32),
                        pltpu.VMEM((NSA_REP, TK, TQ), BF16)],
        compiler_params=_cparams(("parallel", "arbitrary")),
        name="nsa_slc_swa",
    )(q_t, mask_t, k_aug, v_t, proj, v_t, bias_win, rel_bias, gate_t, o_cmp_t)


def _xattn_kernel(hb_ref, hf_ref, wq_ref, kv_ref, wo_ref, g_ref, b_ref, of_ref, ob_ref):
    q = _dot(hb_ref[...], wq_ref[...]).astype(BF16)
    scale = XA_HEAD_DIM ** -0.5
    outs = []
    for hh in range(XA_HEADS):
        lo = hh * XA_HEAD_DIM
        k = kv_ref[:, lo:lo + XA_HEAD_DIM]
        v = kv_ref[:, XA_W + lo:XA_W + lo + XA_HEAD_DIM]
        s = _dot_nt(q[:, lo:lo + XA_HEAD_DIM], k) * scale
        e = jnp.exp(s - jnp.max(s, axis=1, keepdims=True))
        p = e / jnp.sum(e, axis=1, keepdims=True)
        outs.append(_dot(p.astype(BF16), v).astype(BF16))
    o = jnp.concatenate(outs, axis=1)
    y = _layer_norm(ALPHA * hf_ref[...] + _dot(o, wo_ref[...]), g_ref[...], b_ref[...])
    of_ref[...] = y
    ob_ref[...] = y.astype(BF16)


def _xattn(hb, hf, wq, kv, wo, g, b, tm):
    m, n = hf.shape
    tm = min(tm, m)
    return pl.pallas_call(
        _xattn_kernel,
        grid=(m // tm,),
        in_specs=[pl.BlockSpec((tm, n), lambda i: (i, 0)),
                  pl.BlockSpec((tm, n), lambda i: (i, 0)),
                  pl.BlockSpec((n, XA_W), lambda i: (0, 0)),
                  pl.BlockSpec((N_MEM, 2 * XA_W), lambda i: (0, 0)),
                  pl.BlockSpec((XA_W, n), lambda i: (0, 0)),
                  pl.BlockSpec((1, n), lambda i: (0, 0)),
                  pl.BlockSpec((1, n), lambda i: (0, 0))],
        out_specs=[pl.BlockSpec((tm, n), lambda i: (i, 0)),
                   pl.BlockSpec((tm, n), lambda i: (i, 0))],
        out_shape=[jax.ShapeDtypeStruct((m, n), F32), jax.ShapeDtypeStruct((m, n), BF16)],
        compiler_params=_cparams(("parallel",)),
        name="xattn",
    )(hb, hf, wq, kv, wo, g.reshape(1, n), b.reshape(1, n))


def _ffn_up_kernel(h_ref, wa_ref, wg_ref, cwa_ref, cwg_ref, cba_ref, cbg_ref, o_ref, ta_ref, tg_ref):
    i = pl.program_id(1)

    @pl.when(i == 0)
    def _():
        ta_ref[...] = jnp.zeros_like(ta_ref)
        tg_ref[...] = jnp.zeros_like(tg_ref)

    hb = h_ref[...]
    tm = hb.shape[0]
    row = lax.broadcasted_iota(jnp.int32, (tm, o_ref.shape[1]), 0)

    def conv(w_ref, cw_ref, cb_ref, tail_ref):
        u = _dot(hb, w_ref[...])
        tail = tail_ref[...]
        u1 = jnp.where(row == 0, tail[SUBLANE - 1:SUBLANE, :], pltpu.roll(u, 1, 0))
        u2 = pltpu.roll(u, 2, 0)
        u2 = jnp.where(row == 0, tail[SUBLANE - 2:SUBLANE - 1, :], u2)
        u2 = jnp.where(row == 1, tail[SUBLANE - 1:SUBLANE, :], u2)
        tail_ref[...] = u[tm - SUBLANE:, :]
        return cw_ref[0:1, :] * u2 + cw_ref[1:2, :] * u1 + cw_ref[2:3, :] * u + cb_ref[...]

    a = conv(wa_ref, cwa_ref, cba_ref, ta_ref)
    gg = conv(wg_ref, cwg_ref, cbg_ref, tg_ref)
    o_ref[...] = (a * _gelu(gg)).astype(o_ref.dtype)


def _ffn_up(hb, w_up, conv_w, conv_b, tm, tn):
    m, k = hb.shape
    tm = min(tm, m)
    nj = D_FF // tn
    cb = conv_b.reshape(1, 2 * D_FF)
    return pl.pallas_call(
        _ffn_up_kernel,
        grid=(nj, m // tm),
        in_specs=[pl.BlockSpec((tm, k), lambda j, i: (i, 0)),
                  pl.BlockSpec((k, tn), lambda j, i: (0, j)),
                  pl.BlockSpec((k, tn), lambda j, i: (0, nj + j)),
                  pl.BlockSpec((3, tn), lambda j, i: (0, j)),
                  pl.BlockSpec((3, tn), lambda j, i: (0, nj + j)),
                  pl.BlockSpec((1, tn), lambda j, i: (0, j)),
                  pl.BlockSpec((1, tn), lambda j, i: (0, nj + j))],
        out_specs=pl.BlockSpec((tm, tn), lambda j, i: (i, j)),
        out_shape=jax.ShapeDtypeStruct((m, D_FF), BF16),
        scratch_shapes=[pltpu.VMEM((SUBLANE, tn), F32), pltpu.VMEM((SUBLANE, tn), F32)],
        compiler_params=_cparams(("parallel", "arbitrary")),
        name="ffn_up",
    )(hb, w_up, w_up, conv_w, conv_w, cb, cb)


def _static_tables(t_len):
    n16 = t_len // CMP_STRIDE
    n_sel = t_len // SEL_BLOCK
    n_cmp = (t_len - CMP_LEN) // CMP_STRIDE + 1
    cmp_start = np.arange(n16) * CMP_STRIDE
    cmp_end = cmp_start + CMP_LEN - 1
    sel = np.arange(n_sel)
    overlap_t = ((cmp_start[None, :] < (sel[:, None] + 1) * SEL_BLOCK)
                 & (cmp_end[None, :] >= sel[:, None] * SEL_BLOCK)
                 & (np.arange(n16)[None, :] < n_cmp))
    overlap_cat = np.concatenate([overlap_t, overlap_t], axis=1)
    tri_t = np.arange(TK)[None, :] > np.arange(TK)[:, None]
    tri_cat = np.concatenate([tri_t, tri_t], axis=1)
    onehot = (np.arange(t_len)[:, None] // SEL_BLOCK) == sel[None, :]
    return (jnp.asarray(overlap_cat, BF16), jnp.asarray(tri_cat, BF16), jnp.asarray(onehot, BF16))


def _split_in_proj(w):
    sizes = (S5_CH, 512, 512, 512, 1024, 256, 256, 256, 256, 256, 256, 3 * NSA_HEADS)
    offs = np.cumsum((0,) + sizes)
    u5, sbq, sbk, sbv, nq, kc, vc, ksl, vsl, ksw, vsw, gt = (
        w[:, int(offs[n]):int(offs[n + 1])] for n in range(len(sizes)))
    w_nat = jnp.concatenate([sbk, kc, vc, ksl, ksw], axis=1).astype(BF16)
    w_q = jnp.concatenate([sbq, nq], axis=1).T.astype(BF16)
    w_v = jnp.concatenate([sbv, vsl, vsw], axis=1).T.astype(BF16)
    return u5.astype(BF16), w_nat, w_q, w_v, _gate_weights_t(gt)


def kernel(x, mem, w_in, w_out, s5_lambda_re, s5_lambda_im, s5_log_dt, s5_b_re, s5_b_im, s5_c_re, s5_c_im, s5_d, s5_w_glu, s5_b_glu, nsa_cmp_pos, nsa_cmp_w1, nsa_cmp_w2, rel_bias, xa_wq, xa_wkv, xa_wo, ffn_w_up, ffn_conv_w, ffn_conv_b, ffn_w_down, ln_g, ln_b):
    t_len = x.shape[1]
    overlap_cat, tri_cat, onehot = _static_tables(t_len)
    bias_win, bias_cmp = _bias_tables(rel_bias)
    mem_b = mem[0].astype(BF16)
    hf = x[0]
    hb = hf.astype(BF16)
    for l in range(DEPTH):
        w_u5, w_nat, w_q, w_v, w_gate_t = _split_in_proj(w_in[l])
        u5 = _mm(hb, w_u5, F32, 512, 512)
        proj = _mm_heads(hb, w_nat, 512)
        q_t = _mm_nt(w_q, hb, BF16, 512, TQ)
        v_t = _mm_nt(w_v, hb, BF16, 512, TK)
        gate_t = _mm_nt(w_gate_t, hb, F32, HEAD_DIM, TQ)

        s5p = _s5_params(s5_lambda_re[l], s5_lambda_im[l], s5_log_dt[l], s5_b_re[l], s5_b_im[l],
                         s5_c_re[l], s5_c_im[l])
        y_s5 = _s5(u5, s5p, s5_w_glu[l], s5_b_glu[l], s5_d[l])
        y_sb = _stick_breaking(proj, q_t, v_t, tri_cat)

        kvc, kvc_t = _compress(proj, HN_KC, nsa_cmp_pos[l], nsa_cmp_w1[l], nsa_cmp_w2[l])
        o_cmp_t, mask_t = _cmp_select(q_t, kvc, kvc_t, bias_cmp, rel_bias, overlap_cat)
        y_nsa = _slc_swa(proj, q_t, v_t, mask_t, _key_augment(proj, onehot), bias_win, rel_bias,
                         gate_t, o_cmp_t)

        y = jnp.concatenate([y_s5, y_sb, y_nsa], axis=1)
        hf, hb = _mm_ln(y, w_out[l].astype(BF16), hf, ln_g[l, 0], ln_b[l, 0], 512, 1024)

        kv = _mm(mem_b, xa_wkv[l].astype(BF16), BF16, N_MEM, 512)
        hf, hb = _xattn(hb, hf, xa_wq[l].astype(BF16), kv, xa_wo[l].astype(BF16),
                        ln_g[l, 1], ln_b[l, 1], 256)

        act = _ffn_up(hb, ffn_w_up[l].astype(BF16), ffn_conv_w[l], ffn_conv_b[l], 1024, 512)
        hf, hb = _mm_ln(act, ffn_w_down[l].astype(BF16), hf, ln_g[l, 2], ln_b[l, 2], 512, 512)
    return hf[None]
```

```python
import functools
import math

import numpy as np
import jax
import jax.numpy as jnp
from jax import lax
from jax.experimental import pallas as pl
from jax.experimental.pallas import tpu as pltpu

F32 = jnp.float32
BF16 = jnp.bfloat16

D_MODEL = 2048
DEPTH = 4
HEAD_DIM = 64
S5_CH = 512
S5_GROUP = 16
S5_NG = 32
S5_P = 64
SB_HEADS = 8
NSA_HEADS = 16
NSA_KV = 4
NSA_REP = 4
CMP_LEN = 32
CMP_STRIDE = 16
CMP_HID = 128
SEL_BLOCK = 64
SEL_TOPN = 16
WINDOW = 512
FORCE_SCORE = 1e4
N_BUCKETS = 32
MAX_DIST = 1024
N_MEM = 256
XA_HEADS = 4
XA_HEAD_DIM = 128
XA_W = 512
D_FF = 5632
ALPHA = (2.0 * DEPTH) ** 0.25
LN_EPS = 1e-5
NEG = -1e30

LANE = 128
SUBLANE = 8
MXU_DIM = 256
VMEM_LIMIT = 56 * 1024 * 1024
TQ = 512
TK = 256
SUB_Q = TQ // LANE
SUB_K = TK // LANE
HEADS_PER_STEP = 4
EXP_ROWS = 32
S5_SEG = 64
S5_CHUNK = SUBLANE * S5_SEG
S5_CB = 8
BIAS_SUB = LANE
FAR_SUB = MAX_DIST // BIAS_SUB + 1
N_BIAS_SUB = FAR_SUB + SUB_K + SUB_Q - 2

HT_SBQ = 0
HT_NQ = HT_SBQ + SB_HEADS
HT_Q_ROWS = (HT_NQ + NSA_HEADS) * HEAD_DIM
HT_SBV = 0
HT_VSL = HT_SBV + SB_HEADS
HT_VSW = HT_VSL + NSA_KV
HT_V_ROWS = (HT_VSW + NSA_KV) * HEAD_DIM
HN_SBK = 0
HN_KC = HN_SBK + SB_HEADS
HN_KSL = HN_KC + 2 * NSA_KV
HN_KSW = HN_KSL + NSA_KV
HN_HEADS = HN_KSW + NSA_KV


def _cparams(sem):
    return pltpu.CompilerParams(dimension_semantics=sem, vmem_limit_bytes=VMEM_LIMIT)


def _dot(a, b):
    return jnp.dot(a, b, preferred_element_type=F32)


def _dot_nt(a, b):
    return lax.dot_general(a, b, (((1,), (1,)), ((), ())), preferred_element_type=F32)


def _split_bf16(x):
    hi = x.astype(BF16)
    lo = (x - hi.astype(F32)).astype(BF16)
    return hi, lo


def _gelu(x):
    c = math.sqrt(2.0 / math.pi)
    return 0.5 * x * (1.0 + jnp.tanh(c * (x + 0.044715 * (x * x * x))))


def _sigmoid(x):
    return 1.0 / (1.0 + jnp.exp(-x))


def _layer_norm(x, g, b):
    mu = jnp.mean(x, axis=-1, keepdims=True)
    xc = x - mu
    var = jnp.mean(xc * xc, axis=-1, keepdims=True)
    return xc * lax.rsqrt(var + LN_EPS) * g + b


def _mm_kernel(a_ref, b_ref, o_ref):
    o_ref[...] = _dot(a_ref[...], b_ref[...]).astype(o_ref.dtype)


def _mm(a, b, out_dtype, tm, tn):
    m, k = a.shape
    n = b.shape[1]
    tm = min(tm, m)
    return pl.pallas_call(
        _mm_kernel,
        grid=(n // tn, m // tm),
        in_specs=[pl.BlockSpec((tm, k), lambda j, i: (i, 0)),
                  pl.BlockSpec((k, tn), lambda j, i: (0, j))],
        out_specs=pl.BlockSpec((tm, tn), lambda j, i: (i, j)),
        out_shape=jax.ShapeDtypeStruct((m, n), out_dtype),
        compiler_params=_cparams(("parallel", "arbitrary")),
        name="mm",
    )(a, b)


def _mm_heads_kernel(a_ref, b_ref, o_ref):
    o = _dot(a_ref[...], b_ref[...]).astype(o_ref.dtype)
    for r in range(o_ref.shape[0]):
        o_ref[r] = o[:, r * HEAD_DIM:(r + 1) * HEAD_DIM]


def _mm_heads(a, b, tm):
    m, k = a.shape
    n = b.shape[1]
    tn = MXU_DIM
    hpt = tn // HEAD_DIM
    tm = min(tm, m)
    return pl.pallas_call(
        _mm_heads_kernel,
        grid=(n // tn, m // tm),
        in_specs=[pl.BlockSpec((tm, k), lambda j, i: (i, 0)),
                  pl.BlockSpec((k, tn), lambda j, i: (0, j))],
        out_specs=pl.BlockSpec((hpt, tm, HEAD_DIM), lambda j, i: (j, i, 0)),
        out_shape=jax.ShapeDtypeStruct((n // HEAD_DIM, m, HEAD_DIM), BF16),
        compiler_params=_cparams(("parallel", "arbitrary")),
        name="mm_heads",
    )(a, b)


def _mm_nt_kernel(w_ref, a_ref, o_ref):
    o_ref[0] = _dot_nt(w_ref[...], a_ref[...]).astype(o_ref.dtype)


def _mm_nt(w_t, a, out_dtype, tn, tm):
    n, k = w_t.shape
    m = a.shape[0]
    return pl.pallas_call(
        _mm_nt_kernel,
        grid=(n // tn, m // tm),
        in_specs=[pl.BlockSpec((tn, k), lambda j, i: (j, 0)),
                  pl.BlockSpec((tm, k), lambda j, i: (i, 0))],
        out_specs=pl.BlockSpec((1, tn, tm), lambda j, i: (i, j, 0)),
        out_shape=jax.ShapeDtypeStruct((m // tm, n, tm), out_dtype),
        compiler_params=_cparams(("parallel", "arbitrary")),
        name="mm_nt",
    )(w_t, a)


def _mm_ln_kernel(a_ref, w_ref, h_ref, g_ref, b_ref, of_ref, ob_ref, acc_ref):
    kk = pl.program_id(1)

    @pl.when(kk == 0)
    def _():
        acc_ref[...] = jnp.zeros_like(acc_ref)

    acc_ref[...] += _dot(a_ref[...], w_ref[...])

    @pl.when(kk == pl.num_programs(1) - 1)
    def _():
        y = _layer_norm(ALPHA * h_ref[...] + acc_ref[...], g_ref[...], b_ref[...])
        of_ref[...] = y
        ob_ref[...] = y.astype(BF16)


def _mm_ln(a, w, h, g, b, tm, tk):
    m, k = a.shape
    n = w.shape[1]
    tm = min(tm, m)
    return pl.pallas_call(
        _mm_ln_kernel,
        grid=(m // tm, k // tk),
        in_specs=[pl.BlockSpec((tm, tk), lambda i, kk: (i, kk)),
                  pl.BlockSpec((tk, n), lambda i, kk: (kk, 0)),
                  pl.BlockSpec((tm, n), lambda i, kk: (i, 0)),
                  pl.BlockSpec((1, n), lambda i, kk: (0, 0)),
                  pl.BlockSpec((1, n), lambda i, kk: (0, 0))],
        out_specs=[pl.BlockSpec((tm, n), lambda i, kk: (i, 0)),
                   pl.BlockSpec((tm, n), lambda i, kk: (i, 0))],
        out_shape=[jax.ShapeDtypeStruct((m, n), F32), jax.ShapeDtypeStruct((m, n), BF16)],
        scratch_shapes=[pltpu.VMEM((tm, n), F32)],
        compiler_params=_cparams(("parallel", "arbitrary")),
        name="mm_ln",
    )(a, w, h, g.reshape(1, n), b.reshape(1, n))


def _s5_kernel(u_ref, bm_ref, cm_ref, a_ref, d_ref, z_ref, s_ref, p_ref, carry_ref):
    tc = pl.program_id(1)
    w = s_ref.shape[1] // 2
    n_step = p_ref.shape[0]
    ar1 = a_ref[0:1, :]
    ai1 = a_ref[1:2, :]

    @pl.when(tc == 0)
    def _():
        carry_ref[...] = jnp.zeros_like(carry_ref)
        p_ref[0:1, 0:w] = ar1
        p_ref[0:1, w:2 * w] = ai1

        def pw(t, c):
            pr, pi = c
            nr = pr * ar1 - pi * ai1
            ni = pr * ai1 + pi * ar1
            p_ref[pl.ds(t, 1), 0:w] = nr
            p_ref[pl.ds(t, 1), w:2 * w] = ni
            return nr, ni

        lax.fori_loop(1, n_step, pw, (ar1, ai1))

    u = u_ref[...]
    s_ref[...] = _dot(u.astype(BF16), bm_ref[0])

    ar = jnp.broadcast_to(ar1, (SUBLANE, w))
    ai = jnp.broadcast_to(ai1, (SUBLANE, w))

    def step(t, c):
        sr, si = c
        r0 = pl.multiple_of(t * SUBLANE, SUBLANE)
        xr = s_ref[pl.ds(r0, SUBLANE), 0:w]
        xi = s_ref[pl.ds(r0, SUBLANE), w:2 * w]
        nr = ar * sr - ai * si + xr
        ni = ar * si + ai * sr + xi
        s_ref[pl.ds(r0, SUBLANE), 0:w] = nr
        s_ref[pl.ds(r0, SUBLANE), w:2 * w] = ni
        return nr, ni

    zero = jnp.zeros((SUBLANE, w), F32)
    fr, fi = lax.fori_loop(0, n_step, step, (zero, zero), unroll=8)

    amr = p_ref[n_step - 1:n_step, 0:w]
    ami = p_ref[n_step - 1:n_step, w:2 * w]
    row = lax.broadcasted_iota(jnp.int32, (SUBLANE, w), 0)
    cr = carry_ref[0:1, 0:w]
    ci = carry_ref[0:1, w:2 * w]
    cin_r = jnp.zeros((SUBLANE, w), F32)
    cin_i = jnp.zeros((SUBLANE, w), F32)
    for seg in range(SUBLANE):
        cin_r = jnp.where(row == seg, cr, cin_r)
        cin_i = jnp.where(row == seg, ci, cin_i)
        nr = fr[seg:seg + 1, :] + amr * cr - ami * ci
        ni = fi[seg:seg + 1, :] + amr * ci + ami * cr
        cr, ci = nr, ni
    carry_ref[0:1, 0:w] = cr
    carry_ref[0:1, w:2 * w] = ci

    def fix(t, c):
        r0 = pl.multiple_of(t * SUBLANE, SUBLANE)
        pr = p_ref[pl.ds(t, 1), 0:w]
        pi = p_ref[pl.ds(t, 1), w:2 * w]
        s_ref[pl.ds(r0, SUBLANE), 0:w] += pr * cin_r - pi * cin_i
        s_ref[pl.ds(r0, SUBLANE), w:2 * w] += pr * cin_i + pi * cin_r
        return c

    lax.fori_loop(0, n_step, fix, 0, unroll=8)

    y = _dot(s_ref[...].astype(BF16), cm_ref[0]) + d_ref[...] * u
    z_ref[...] = _gelu(y)


def _s5_glu_kernel(z_ref, w_ref, b_ref, o_ref):
    z = z_ref[...]
    gate = _sigmoid(_dot(z.astype(BF16), w_ref[...]) + b_ref[...])
    o_ref[...] = (z * gate).astype(o_ref.dtype)


def _s5_params(lam_re, lam_im, log_dt, b_re, b_im, c_re, c_im):
    delta = jnp.exp(log_dt)[:, None]
    mag = jnp.exp(lam_re * delta)
    ar = mag * jnp.cos(lam_im * delta)
    ai = mag * jnp.sin(lam_im * delta)
    den = lam_re * lam_re + lam_im * lam_im
    cr = ((ar - 1.0) * lam_re + ai * lam_im) / den
    ci = (ai * lam_re - (ar - 1.0) * lam_im) / den
    br = cr[..., None] * b_re - ci[..., None] * b_im
    bi = cr[..., None] * b_im + ci[..., None] * b_re
    ncb = S5_NG // S5_CB
    eye = jnp.eye(S5_CB, dtype=F32)

    def in_mat(b):
        b = b.reshape(ncb, S5_CB, S5_P, S5_GROUP)
        m = jnp.einsum('ngpc,gh->ngchp', b, eye)
        return m.reshape(ncb, S5_CB * S5_GROUP, S5_CB * S5_P)

    def out_mat(c):
        c = c.reshape(ncb, S5_CB, S5_GROUP, S5_P)
        m = jnp.einsum('ngcp,gh->ngphc', c, eye)
        return m.reshape(ncb, S5_CB * S5_P, S5_CB * S5_GROUP)

    bm = jnp.concatenate([in_mat(br), in_mat(bi)], axis=2).astype(BF16)
    cm = jnp.concatenate([out_mat(c_re), -out_mat(c_im)], axis=1).astype(BF16)
    a = jnp.stack([ar.reshape(-1), ai.reshape(-1)], axis=0)
    return bm, cm, a


def _s5(u, p, w_glu, b_glu, d):
    t_len = u.shape[0]
    nc = t_len // S5_CHUNK
    ncb = S5_NG // S5_CB
    wl = S5_CB * S5_GROUP
    ws = S5_CB * S5_P
    bm, cm, a = p
    up = u.reshape(nc, SUBLANE, S5_SEG, S5_CH).transpose(0, 2, 1, 3).reshape(t_len, S5_CH)
    z = pl.pallas_call(
        _s5_kernel,
        grid=(ncb, nc),
        in_specs=[pl.BlockSpec((S5_CHUNK, wl), lambda c, t: (t, c)),
                  pl.BlockSpec((1, wl, 2 * ws), lambda c, t: (c, 0, 0)),
                  pl.BlockSpec((1, 2 * ws, wl), lambda c, t: (c, 0, 0)),
                  pl.BlockSpec((2, ws), lambda c, t: (0, c)),
                  pl.BlockSpec((1, wl), lambda c, t: (0, c))],
        out_specs=pl.BlockSpec((S5_CHUNK, wl), lambda c, t: (t, c)),
        out_shape=jax.ShapeDtypeStruct((t_len, S5_CH), F32),
        scratch_shapes=[pltpu.VMEM((S5_CHUNK, 2 * ws), F32),
                        pltpu.VMEM((S5_SEG, 2 * ws), F32),
                        pltpu.VMEM((SUBLANE, 2 * ws), F32)],
        compiler_params=_cparams(("parallel", "arbitrary")),
        name="s5_scan",
    )(up, bm, cm, a, d.reshape(1, S5_CH))
    tm = min(1024, t_len)
    y = pl.pallas_call(
        _s5_glu_kernel,
        grid=(t_len // tm,),
        in_specs=[pl.BlockSpec((tm, S5_CH), lambda i: (i, 0)),
                  pl.BlockSpec((S5_CH, S5_CH), lambda i: (0, 0)),
                  pl.BlockSpec((1, S5_CH), lambda i: (0, 0))],
        out_specs=pl.BlockSpec((tm, S5_CH), lambda i: (i, 0)),
        out_shape=jax.ShapeDtypeStruct((t_len, S5_CH), BF16),
        compiler_params=_cparams(("parallel",)),
        name="s5_glu",
    )(z, w_glu.astype(BF16), b_glu.reshape(1, S5_CH))
    return y.reshape(nc, S5_SEG, SUBLANE, S5_CH).transpose(0, 2, 1, 3).reshape(t_len, S5_CH)


def _sb_kernel(q_ref, k_ref, v_ref, u_ref, o_ref, r_ref, acc_ref, nl_ref, arg_ref, w_ref):
    qi = pl.program_id(1)
    ucat = u_ref[...]
    qs = [q_ref[0, r * HEAD_DIM:(r + 1) * HEAD_DIM, :] * 0.125 for r in range(HEADS_PER_STEP)]
    r_ref[...] = jnp.zeros_like(r_ref)
    acc_ref[...] = jnp.zeros_like(acc_ref)
    key = lax.broadcasted_iota(jnp.int32, (TK, TQ), 0)
    qry = lax.broadcasted_iota(jnp.int32, (TK, TQ), 1)

    def tile(kt, diag_off):
        k0 = pl.multiple_of(kt * TK, TK)
        mask = None if diag_off is None else (key + diag_off) < qry
        sums = []
        for r in range(HEADS_PER_STEP):
            z = _dot(k_ref[r, pl.ds(k0, TK), :], qs[r])
            nl = jnp.maximum(z, 0.0) + jnp.log(1.0 + jnp.exp(-jnp.abs(z)))
            arg_ref[r] = z - nl
            if mask is not None:
                nl = jnp.where(mask, nl, 0.0)
            hi, lo = _split_bf16(nl)
            nl_ref[r, 0:TK, :] = hi
            nl_ref[r, TK:2 * TK, :] = lo
            sums.append(jnp.sum(nl, axis=0, keepdims=True))
        for r in range(HEADS_PER_STEP):
            later = _dot(ucat, nl_ref[r])
            w = jnp.exp(arg_ref[r] - later)
            if mask is not None:
                w = jnp.where(mask, w, 0.0)
            w_ref[r] = w.astype(BF16)
        for r in range(HEADS_PER_STEP):
            v = v_ref[kt, r * HEAD_DIM:(r + 1) * HEAD_DIM, :]
            r_old = r_ref[r, 0:1, :]
            acc_ref[r] += _dot(v, w_ref[r]) * jnp.exp(-r_old)
            r_ref[r, 0:1, :] = r_old + sums[r]

    n_diag = TQ // TK
    for c in reversed(range(n_diag)):
        tile(n_diag * qi + c, c * TK)

    def alive():
        return (jnp.max(jnp.exp(-r_ref[:, 0:1, :])) > 0.0).astype(jnp.int32)

    def cond(state):
        return (state[0] >= 0) & (state[1] > 0)

    def body(state):
        tile(state[0], None)
        return state[0] - 1, alive()

    lax.while_loop(cond, body, (n_diag * qi - 1, alive()))
    o_t = jnp.concatenate([acc_ref[r] for r in range(HEADS_PER_STEP)], axis=0)
    o_ref[...] = o_t.T.astype(o_ref.dtype)


def _stick_breaking(proj, q_t, v_t, ucat):
    t_len = proj.shape[1]
    nt = t_len // TQ
    hw = HEADS_PER_STEP * HEAD_DIM
    return pl.pallas_call(
        _sb_kernel,
        grid=(SB_HEADS // HEADS_PER_STEP, nt),
        in_specs=[pl.BlockSpec((1, hw, TQ), lambda h, i: (i, HT_SBQ // HEADS_PER_STEP + h, 0)),
                  pl.BlockSpec((HEADS_PER_STEP, t_len, HEAD_DIM),
                               lambda h, i: (HN_SBK // HEADS_PER_STEP + h, 0, 0)),
                  pl.BlockSpec((t_len // TK, hw, TK),
                               lambda h, i: (0, HT_SBV // HEADS_PER_STEP + h, 0)),
                  pl.BlockSpec((TK, 2 * TK), lambda h, i: (0, 0))],
        out_specs=pl.BlockSpec((TQ, hw), lambda h, i: (i, h)),
        out_shape=jax.ShapeDtypeStruct((t_len, SB_HEADS * HEAD_DIM), BF16),
        scratch_shapes=[pltpu.VMEM((HEADS_PER_STEP, SUBLANE, TQ), F32),
                        pltpu.VMEM((HEADS_PER_STEP, HEAD_DIM, TQ), F32),
                        pltpu.VMEM((HEADS_PER_STEP, 2 * TK, TQ), BF16),
                        pltpu.VMEM((HEADS_PER_STEP, TK, TQ), F32),
                        pltpu.VMEM((HEADS_PER_STEP, TK, TQ), BF16)],
        compiler_params=_cparams(("parallel", "arbitrary")),
        name="stick_breaking",
    )(q_t, proj, v_t, ucat)


def _t5_bucket(dist):
    n = jnp.maximum(dist, 0)
    max_exact = N_BUCKETS // 2
    nf = jnp.maximum(n, 1).astype(jnp.float32)
    large = max_exact + (jnp.log(nf / max_exact) / math.log(MAX_DIST / max_exact)
                         * (N_BUCKETS - max_exact)).astype(jnp.int32)
    large = jnp.minimum(large, N_BUCKETS - 1)
    return jnp.where(n < max_exact, n, large)


def _bias_lookup_kernel(tab_ref, idx_ref, o_ref):
    h = pl.program_id(0)
    idx = idx_ref[...]
    acc = jnp.zeros(idx.shape, F32)
    for b in range(N_BUCKETS):
        acc = jnp.where(idx == b, tab_ref[b, h], acc)
    o_ref[0] = acc


def _bias_lookup(tab, idx):
    r, c = idx.shape
    n_h = tab.shape[1]
    return pl.pallas_call(
        _bias_lookup_kernel,
        grid=(n_h,),
        in_specs=[pl.BlockSpec(memory_space=pltpu.SMEM),
                  pl.BlockSpec((r, c), lambda h: (0, 0))],
        out_specs=pl.BlockSpec((1, r, c), lambda h: (h, 0, 0)),
        out_shape=jax.ShapeDtypeStruct((n_h, r, c), F32),
        compiler_params=_cparams(("arbitrary",)),
        name="bias_lookup",
    )(tab, idx)


def _bias_tables(rel_bias):
    i = jnp.arange(BIAS_SUB)
    d_win = (BIAS_SUB * jnp.arange(N_BIAS_SUB)[:, None, None] + i[None, None, :] - i[None, :, None])
    idx_win = _t5_bucket(d_win).reshape(N_BIAS_SUB * BIAS_SUB, BIAS_SUB)
    win = _bias_lookup(rel_bias, idx_win).reshape(NSA_HEADS, N_BIAS_SUB, BIAS_SUB, BIAS_SUB)
    n_m = (LANE * CMP_STRIDE) // TQ
    m = jnp.arange(n_m)[:, None, None]
    c = jnp.arange(2 * LANE)[None, :, None]
    qi = jnp.arange(TQ)[None, None, :]
    d_cmp = TQ * m + qi + LANE * CMP_STRIDE - CMP_STRIDE * c - (CMP_LEN - 1)
    idx_cmp = _t5_bucket(d_cmp).reshape(n_m * 2 * LANE, TQ)
    cmp_t = _bias_lookup(rel_bias, idx_cmp).reshape(NSA_HEADS, n_m, 2 * LANE, TQ)
    return win, cmp_t


def _compress_kernel(c0_ref, c1_ref, w1_ref, pos_ref, w2_ref, w2t_ref, o_ref, ot_ref):
    half = c0_ref.shape[3]
    w1 = w1_ref[0]
    posb = _dot(pos_ref[0], w1)[0:1, :]
    hid = _dot(c0_ref[0, 0], w1[0:half, :]) + _dot(c1_ref[0, 0], w1[half:, :]) + posb
    act = _gelu(hid).astype(BF16)
    o_ref[0, 0] = _dot(act, w2_ref[0]).astype(o_ref.dtype)
    ot_ref[0, 0] = _dot_nt(w2t_ref[0], act).astype(ot_ref.dtype)


def _compress(proj, head0, cmp_pos, cmp_w1, cmp_w2):
    t_len = proj.shape[1]
    n16 = t_len // CMP_STRIDE
    half = CMP_STRIDE * HEAD_DIM
    kv = proj[head0:head0 + 2 * NSA_KV].reshape(2, NSA_KV, n16, half)
    kv_next = jnp.concatenate([kv[:, :, 1:], jnp.zeros((2, NSA_KV, 1, half), BF16)], axis=2)
    pos = jnp.broadcast_to(cmp_pos.reshape(2, 1, CMP_LEN * HEAD_DIM), (2, SUBLANE, CMP_LEN * HEAD_DIM))
    w2 = cmp_w2.astype(BF16)
    return pl.pallas_call(
        _compress_kernel,
        grid=(2, NSA_KV),
        in_specs=[pl.BlockSpec((1, 1, n16, half), lambda j, g: (j, g, 0, 0)),
                  pl.BlockSpec((1, 1, n16, half), lambda j, g: (j, g, 0, 0)),
                  pl.BlockSpec((1, 2 * half, CMP_HID), lambda j, g: (j, 0, 0)),
                  pl.BlockSpec((1, SUBLANE, 2 * half), lambda j, g: (j, 0, 0)),
                  pl.BlockSpec((1, CMP_HID, HEAD_DIM), lambda j, g: (j, 0, 0)),
                  pl.BlockSpec((1, HEAD_DIM, CMP_HID), lambda j, g: (j, 0, 0))],
        out_specs=[pl.BlockSpec((1, 1, n16, HEAD_DIM), lambda j, g: (j, g, 0, 0)),
                   pl.BlockSpec((1, 1, HEAD_DIM, n16), lambda j, g: (j, g, 0, 0))],
        out_shape=[jax.ShapeDtypeStruct((2, NSA_KV, n16, HEAD_DIM), BF16),
                   jax.ShapeDtypeStruct((2, NSA_KV, HEAD_DIM, n16), BF16)],
        compiler_params=_cparams(("parallel", "arbitrary")),
        name="nsa_compress",
    )(kv, kv_next, cmp_w1.astype(BF16), pos.astype(BF16), w2, w2.transpose(0, 2, 1))


def _cmp_select_kernel(q_ref, kc_ref, vct_ref, bias_ref, tab_ref, ov_ref, oc_ref, mn_ref, *, n_cmp):
    g = pl.program_id(0)
    qi = pl.program_id(1)
    n16 = kc_ref.shape[2]
    n_ct = n16 // LANE
    kc = kc_ref[0, 0]
    vct = vct_ref[0, 0]
    q_per_ct = (LANE * CMP_STRIDE) // TQ
    jd = qi // q_per_ct
    n = lax.broadcasted_iota(jnp.int32, (n16, TQ), 0)
    t = qi * TQ + lax.broadcasted_iota(jnp.int32, (n16, TQ), 1)
    valid = (CMP_STRIDE * n + (CMP_LEN - 1) <= t) & (n < n_cmp)
    psum = jnp.zeros((n16, TQ), F32)
    for r in range(NSA_REP):
        q = q_ref[0, r * HEAD_DIM:(r + 1) * HEAD_DIM, :] * 0.125
        far = tab_ref[N_BUCKETS - 1, g * NSA_REP + r]
        rows = []
        for jc in range(n_ct):
            near = jnp.where(jc == jd, bias_ref[r, 0, LANE:2 * LANE, :], bias_ref[r, 0, 0:LANE, :])
            rows.append(jnp.where((jc == jd) | (jc == jd - 1), near, far))
        bias = rows[0] if n_ct == 1 else jnp.concatenate(rows, axis=0)
        s = jnp.where(valid, _dot(kc, q) + bias, NEG)
        mx = jnp.max(s, axis=0, keepdims=True)
        e = jnp.where(valid, jnp.exp(s - mx), 0.0)
        p = e * (1.0 / jnp.maximum(jnp.sum(e, axis=0, keepdims=True), 1e-30))
        psum = psum + p
        oc_ref[r * HEAD_DIM:(r + 1) * HEAD_DIM, :] = _dot(vct, p.astype(BF16))
    hi, lo = _split_bf16(psum)
    imp = _dot(ov_ref[...], jnp.concatenate([hi, lo], axis=0))
    n_sel = imp.shape[0]
    j = lax.broadcasted_iota(jnp.int32, (n_sel, LANE), 0)
    jf = j.astype(F32)
    for cc in range(TQ // LANE):
        tq = qi * TQ + cc * LANE + lax.broadcasted_iota(jnp.int32, (n_sel, LANE), 1)
        cur = tq // SEL_BLOCK
        forced = (j == 0) | (j == cur) | (j == cur - 1)
        score = jnp.where(j <= cur,
                          jnp.where(forced, FORCE_SCORE, imp[:, cc * LANE:(cc + 1) * LANE]), -1.0)
        sel = jnp.zeros((n_sel, LANE), F32)
        for _ in range(min(SEL_TOPN, n_sel)):
            mx = jnp.max(score, axis=0, keepdims=True)
            first = jnp.min(jnp.where(score == mx, jf, float(n_sel)), axis=0, keepdims=True)
            hit = jf == first
            sel = jnp.where(hit, 1.0, sel)
            score = jnp.where(hit, -jnp.inf, score)
        mn_ref[0, :, cc * LANE:(cc + 1) * LANE] = (sel - 1.0).astype(mn_ref.dtype)


def _cmp_select(q_t, kc, vct, bias_cmp, rel_bias, overlap_cat):
    nt = q_t.shape[0]
    t_len = nt * TQ
    n16 = kc.shape[2]
    n_sel = t_len // SEL_BLOCK
    n_cmp = (t_len - CMP_LEN) // CMP_STRIDE + 1
    q_per_ct = (LANE * CMP_STRIDE) // TQ
    hw = NSA_REP * HEAD_DIM
    return pl.pallas_call(
        functools.partial(_cmp_select_kernel, n_cmp=n_cmp),
        grid=(NSA_KV, nt),
        in_specs=[pl.BlockSpec((1, hw, TQ), lambda g, i: (i, HT_NQ // NSA_REP + g, 0)),
                  pl.BlockSpec((1, 1, n16, HEAD_DIM), lambda g, i: (0, g, 0, 0)),
                  pl.BlockSpec((1, 1, HEAD_DIM, n16), lambda g, i: (1, g, 0, 0)),
                  pl.BlockSpec((NSA_REP, 1, 2 * LANE, TQ), lambda g, i: (g, i % q_per_ct, 0, 0)),
                  pl.BlockSpec(memory_space=pltpu.SMEM),
                  pl.BlockSpec((n_sel, 2 * n16), lambda g, i: (0, 0))],
        out_specs=[pl.BlockSpec((hw, TQ), lambda g, i: (g, i)),
                   pl.BlockSpec((1, n_sel, TQ), lambda g, i: (g, 0, i))],
        out_shape=[jax.ShapeDtypeStruct((NSA_HEADS * HEAD_DIM, t_len), F32),
                   jax.ShapeDtypeStruct((NSA_KV, n_sel, t_len), BF16)],
        compiler_params=_cparams(("parallel", "arbitrary")),
        name="nsa_cmp_select",
    )(q_t, kc, vct, bias_cmp, rel_bias, overlap_cat)


def _bias_tile(win_ref, r, base):
    rows = []
    for b in range(SUB_K):
        cols = []
        for a in range(SUB_Q):
            idx = jnp.maximum(base + a - b, 0)
            cols.append(win_ref[r, idx])
        rows.append(jnp.concatenate(cols, axis=1))
    return jnp.concatenate(rows, axis=0)


def _slc_swa_kernel(q_ref, mn_ref, ka_ref, vs_ref, kw_ref, vw_ref, win_ref, tab_ref, gate_ref,
                    oc_ref, o_ref, qa_ref, m_ref, l_ref, acc_ref, s_ref, p_ref):
    g = pl.program_id(0)
    qi = pl.program_id(1)
    n_sel = mn_ref.shape[1]
    n_diag = TQ // TK
    big = mn_ref[0] * jnp.asarray(1e30, BF16)
    for r in range(NSA_REP):
        qa_ref[r, 0:HEAD_DIM, :] = q_ref[0, r * HEAD_DIM:(r + 1) * HEAD_DIM, :] * 0.125
        qa_ref[r, HEAD_DIM:LANE, :] = jnp.zeros((LANE - HEAD_DIM, TQ), BF16)
        qa_ref[r, LANE:LANE + n_sel, :] = big
        if n_sel < LANE:
            qa_ref[r, LANE + n_sel:2 * LANE, :] = jnp.zeros((LANE - n_sel, TQ), BF16)
    m_ref[...] = jnp.full_like(m_ref, NEG)
    l_ref[...] = jnp.zeros_like(l_ref)
    acc_ref[...] = jnp.zeros_like(acc_ref)
    key = lax.broadcasted_iota(jnp.int32, (TK, TQ), 0)
    qry = lax.broadcasted_iota(jnp.int32, (TK, TQ), 1)

    def scores(kt, far, diag_off, buf):
        k0 = pl.multiple_of(kt * TK, TK)
        ka = ka_ref[0, pl.ds(k0, TK), :]
        for r in range(NSA_REP):
            s = _dot(ka, qa_ref[r])
            if not far:
                s = s + _bias_tile(win_ref, r, SUB_Q * qi - SUB_K * kt)
            if diag_off is not None:
                s = jnp.where(key + diag_off <= qry, s, NEG)
            s_ref[buf, r] = s

    def consume(kt, buf):
        v = vs_ref[kt]
        stats = []
        groups = EXP_ROWS // SUBLANE

        def rows8(c0):
            return s_ref[buf, r, c0:c0 + EXP_ROWS, :].reshape(groups, SUBLANE, TQ)

        for r in range(NSA_REP):
            mx8 = jnp.max(rows8(0), axis=0)
            for c0 in range(EXP_ROWS, TK, EXP_ROWS):
                mx8 = jnp.maximum(mx8, jnp.max(rows8(c0), axis=0))
            m_old = m_ref[r, 0:1, :]
            m_new = jnp.maximum(m_old, jnp.max(mx8, axis=0, keepdims=True))
            m_ref[r, 0:1, :] = m_new
            stats.append((m_new, jnp.exp(m_old - m_new)))
        for r in range(NSA_REP):
            m_new, alpha = stats[r]
            ps8 = jnp.zeros((SUBLANE, TQ), F32)
            for c0 in range(0, TK, EXP_ROWS):
                p = jnp.exp(s_ref[buf, r, c0:c0 + EXP_ROWS, :] - m_new)
                ps8 = ps8 + jnp.sum(p.reshape(groups, SUBLANE, TQ), axis=0)
                p_ref[r, c0:c0 + EXP_ROWS, :] = p.astype(BF16)
            l_ref[r, 0:1, :] = alpha * l_ref[r, 0:1, :] + jnp.sum(ps8, axis=0, keepdims=True)
        for r in range(NSA_REP):
            acc_ref[r] = stats[r][1] * acc_ref[r] + _dot(v, p_ref[r])

    def slc_tile(kt, far, diag_off):
        scores(kt, far, diag_off, 0)
        consume(kt, 0)

    n_far = jnp.maximum((SUB_Q * qi - (SUB_K - 1) - FAR_SUB) // SUB_K + 1, 0)

    scores(0, True, None, 0)

    def far_body(j, c):
        scores(2 * j + 1, True, None, 1)
        consume(2 * j, 0)
        scores(2 * j + 2, True, None, 0)
        consume(2 * j + 1, 1)
        return c

    lax.fori_loop(0, n_far // 2, far_body, 0)

    @pl.when(n_far % 2 == 1)
    def _():
        consume(n_far - 1, 0)
    for r in range(NSA_REP):
        m_ref[r, 0:1, :] = m_ref[r, 0:1, :] + tab_ref[N_BUCKETS - 1, g * NSA_REP + r]

    n_band = n_diag * qi - n_far
    scores(n_far, False, None, 0)

    def band_body(j, c):
        kt = n_far + 2 * j
        scores(kt + 1, False, None, 1)
        consume(kt, 0)
        scores(kt + 2, False, None, 0)
        consume(kt + 1, 1)
        return c

    lax.fori_loop(0, n_band // 2, band_body, 0)

    @pl.when(n_band % 2 == 1)
    def _():
        consume(n_diag * qi - 1, 0)
    for c in range(n_diag):
        slc_tile(n_diag * qi + c, False, c * TK)

    n_wt = WINDOW // TK + n_diag
    outs = []
    for r in range(NSA_REP):
        q = qa_ref[r, 0:HEAD_DIM, :]
        ss, vv, oks = [], [], []
        for c in range(n_wt):
            back = WINDOW // TK - c
            kt = n_diag * qi - back
            ktc = jnp.maximum(kt, 0)
            k0 = pl.multiple_of(ktc * TK, TK)
            s = _dot(kw_ref[0, pl.ds(k0, TK), :], q) + _bias_tile(win_ref, r, SUB_K * back)
            dist = back * TK + qry - key
            ok = (dist >= 0) & (dist < WINDOW) & (kt >= 0)
            ss.append(jnp.where(ok, s, NEG))
            oks.append(ok)
            vv.append(vw_ref[ktc])
        mx = jnp.max(ss[0], axis=0, keepdims=True)
        for s in ss[1:]:
            mx = jnp.maximum(mx, jnp.max(s, axis=0, keepdims=True))
        den = jnp.zeros((1, TQ), F32)
        o_w = jnp.zeros((HEAD_DIM, TQ), F32)
        for s, v, ok in zip(ss, vv, oks):
            e = jnp.where(ok, jnp.exp(s - mx), 0.0)
            den = den + jnp.sum(e, axis=0, keepdims=True)
            o_w = o_w + _dot(v, e.astype(BF16))
        o_w = o_w * (1.0 / jnp.maximum(den, 1e-30))
        o_s = acc_ref[r] * (1.0 / jnp.maximum(l_ref[r, 0:1, :], 1e-30))
        o_c = oc_ref[r * HEAD_DIM:(r + 1) * HEAD_DIM, :]
        row0 = 3 * (g * NSA_REP + r)
        gates = [_sigmoid(gate_ref[0, pl.ds(row0 + c, 1), :]) for c in range(3)]
        outs.append(gates[0] * o_c + gates[1] * o_s + gates[2] * o_w)
    o_ref[...] = jnp.concatenate(outs, axis=0).T.astype(o_ref.dtype)


def _gate_weights_t(w_gate):
    pad = HEAD_DIM - w_gate.shape[1]
    return jnp.pad(w_gate.T, ((0, pad), (0, 0))).astype(BF16)


def _key_augment(proj, onehot):
    t_len = proj.shape[1]
    n_sel = t_len // SEL_BLOCK
    pad = LANE - n_sel
    return jnp.concatenate(
        [proj[HN_KSL:HN_KSL + NSA_KV],
         jnp.zeros((NSA_KV, t_len, LANE - HEAD_DIM), BF16),
         jnp.broadcast_to(jnp.pad(onehot, ((0, 0), (0, pad)))[None], (NSA_KV, t_len, LANE))], axis=2)


def _slc_swa(proj, q_t, v_t, mask_t, k_aug, bias_win, rel_bias, gate_t, o_cmp_t):
    t_len = proj.shape[1]
    nt = t_len // TQ
    nkt = t_len // TK
    n_sel = t_len // SEL_BLOCK
    hw = NSA_REP * HEAD_DIM
    return pl.pallas_call(
        _slc_swa_kernel,
        grid=(NSA_KV, nt),
        in_specs=[pl.BlockSpec((1, hw, TQ), lambda g, i: (i, HT_NQ // NSA_REP + g, 0)),
                  pl.BlockSpec((1, n_sel, TQ), lambda g, i: (g, 0, i)),
                  pl.BlockSpec((1, t_len, 2 * LANE), lambda g, i: (g, 0, 0)),
                  pl.BlockSpec((nkt, HEAD_DIM, TK), lambda g, i: (0, HT_VSL + g, 0)),
                  pl.BlockSpec((1, t_len, HEAD_DIM), lambda g, i: (HN_KSW + g, 0, 0)),
                  pl.BlockSpec((nkt, HEAD_DIM, TK), lambda g, i: (0, HT_VSW + g, 0)),
                  pl.BlockSpec((NSA_REP, N_BIAS_SUB, BIAS_SUB, BIAS_SUB), lambda g, i: (g, 0, 0, 0)),
                  pl.BlockSpec(memory_space=pltpu.SMEM),
                  pl.BlockSpec((1, HEAD_DIM, TQ), lambda g, i: (i, 0, 0)),
                  pl.BlockSpec((hw, TQ), lambda g, i: (g, i))],
        out_specs=pl.BlockSpec((TQ, hw), lambda g, i: (i, g)),
        out_shape=jax.ShapeDtypeStruct((t_len, NSA_HEADS * HEAD_DIM), BF16),
        scratch_shapes=[pltpu.VMEM((NSA_REP, 2 * LANE, TQ), BF16),
                        pltpu.VMEM((NSA_REP, SUBLANE, TQ), F32),
                        pltpu.VMEM((NSA_REP, SUBLANE, TQ), F32),
                        pltpu.VMEM((NSA_REP, HEAD_DIM, TQ), F32),
                        pltpu.VMEM((2, NSA_REP, TK, TQ), F32),
                        pltpu.VMEM((NSA_REP, TK, TQ), BF16)],
        compiler_params=_cparams(("parallel", "arbitrary")),
        name="nsa_slc_swa",
    )(q_t, mask_t, k_aug, v_t, proj, v_t, bias_win, rel_bias, gate_t, o_cmp_t)


def _xattn_kernel(hb_ref, hf_ref, wq_ref, kv_ref, wo_ref, g_ref, b_ref, of_ref, ob_ref):
    q = _dot(hb_ref[...], wq_ref[...]).astype(BF16)
    scale = XA_HEAD_DIM ** -0.5
    outs = []
    for hh in range(XA_HEADS):
        lo = hh * XA_HEAD_DIM
        k = kv_ref[:, lo:lo + XA_HEAD_DIM]
        v = kv_ref[:, XA_W + lo:XA_W + lo + XA_HEAD_DIM]
        s = _dot_nt(q[:, lo:lo + XA_HEAD_DIM], k) * scale
        e = jnp.exp(s - jnp.max(s, axis=1, keepdims=True))
        p = e / jnp.sum(e, axis=1, keepdims=True)
        outs.append(_dot(p.astype(BF16), v).astype(BF16))
    o = jnp.concatenate(outs, axis=1)
    y = _layer_norm(ALPHA * hf_ref[...] + _dot(o, wo_ref[...]), g_ref[...], b_ref[...])
    of_ref[...] = y
    ob_ref[...] = y.astype(BF16)


def _xattn(hb, hf, wq, kv, wo, g, b, tm):
    m, n = hf.shape
    tm = min(tm, m)
    return pl.pallas_call(
        _xattn_kernel,
        grid=(m // tm,),
        in_specs=[pl.BlockSpec((tm, n), lambda i: (i, 0)),
                  pl.BlockSpec((tm, n), lambda i: (i, 0)),
                  pl.BlockSpec((n, XA_W), lambda i: (0, 0)),
                  pl.BlockSpec((N_MEM, 2 * XA_W), lambda i: (0, 0)),
                  pl.BlockSpec((XA_W, n), lambda i: (0, 0)),
                  pl.BlockSpec((1, n), lambda i: (0, 0)),
                  pl.BlockSpec((1, n), lambda i: (0, 0))],
        out_specs=[pl.BlockSpec((tm, n), lambda i: (i, 0)),
                   pl.BlockSpec((tm, n), lambda i: (i, 0))],
        out_shape=[jax.ShapeDtypeStruct((m, n), F32), jax.ShapeDtypeStruct((m, n), BF16)],
        compiler_params=_cparams(("parallel",)),
        name="xattn",
    )(hb, hf, wq, kv, wo, g.reshape(1, n), b.reshape(1, n))


def _ffn_up_kernel(h_ref, waf_ref, wgf_ref, cwa_ref, cwg_ref, cba_ref, cbg_ref, o_ref, ta_ref, tg_ref,
                   wa_ref, wg_ref):
    i = pl.program_id(1)

    @pl.when(i == 0)
    def _():
        ta_ref[...] = jnp.zeros_like(ta_ref)
        tg_ref[...] = jnp.zeros_like(tg_ref)
        wa_ref[...] = waf_ref[...].astype(BF16)
        wg_ref[...] = wgf_ref[...].astype(BF16)

    hb = h_ref[...]
    tm = hb.shape[0]
    row = lax.broadcasted_iota(jnp.int32, (tm, o_ref.shape[1]), 0)

    def conv(w_ref, cw_ref, cb_ref, tail_ref):
        u = _dot(hb, w_ref[...])
        tail = tail_ref[...]
        u1 = jnp.where(row == 0, tail[SUBLANE - 1:SUBLANE, :], pltpu.roll(u, 1, 0))
        u2 = pltpu.roll(u, 2, 0)
        u2 = jnp.where(row == 0, tail[SUBLANE - 2:SUBLANE - 1, :], u2)
        u2 = jnp.where(row == 1, tail[SUBLANE - 1:SUBLANE, :], u2)
        tail_ref[...] = u[tm - SUBLANE:, :]
        return cw_ref[0:1, :] * u2 + cw_ref[1:2, :] * u1 + cw_ref[2:3, :] * u + cb_ref[...]

    a = conv(wa_ref, cwa_ref, cba_ref, ta_ref)
    gg = conv(wg_ref, cwg_ref, cbg_ref, tg_ref)
    o_ref[...] = (a * _gelu(gg)).astype(o_ref.dtype)


def _ffn_up(hb, w_up, conv_w, conv_b, tm, tn):
    m, k = hb.shape
    tm = min(tm, m)
    nj = D_FF // tn
    cb = conv_b.reshape(1, 2 * D_FF)
    return pl.pallas_call(
        _ffn_up_kernel,
        grid=(nj, m // tm),
        in_specs=[pl.BlockSpec((tm, k), lambda j, i: (i, 0)),
                  pl.BlockSpec((k, tn), lambda j, i: (0, j)),
                  pl.BlockSpec((k, tn), lambda j, i: (0, nj + j)),
                  pl.BlockSpec((3, tn), lambda j, i: (0, j)),
                  pl.BlockSpec((3, tn), lambda j, i: (0, nj + j)),
                  pl.BlockSpec((1, tn), lambda j, i: (0, j)),
                  pl.BlockSpec((1, tn), lambda j, i: (0, nj + j))],
        out_specs=pl.BlockSpec((tm, tn), lambda j, i: (i, j)),
        out_shape=jax.ShapeDtypeStruct((m, D_FF), BF16),
        scratch_shapes=[pltpu.VMEM((SUBLANE, tn), F32), pltpu.VMEM((SUBLANE, tn), F32),
                        pltpu.VMEM((k, tn), BF16), pltpu.VMEM((k, tn), BF16)],
        compiler_params=_cparams(("parallel", "arbitrary")),
        name="ffn_up",
    )(hb, w_up, w_up, conv_w, conv_w, cb, cb)


def _static_tables(t_len):
    n16 = t_len // CMP_STRIDE
    n_sel = t_len // SEL_BLOCK
    n_cmp = (t_len - CMP_LEN) // CMP_STRIDE + 1
    cmp_start = np.arange(n16) * CMP_STRIDE
    cmp_end = cmp_start + CMP_LEN - 1
    sel = np.arange(n_sel)
    overlap_t = ((cmp_start[None, :] < (sel[:, None] + 1) * SEL_BLOCK)
                 & (cmp_end[None, :] >= sel[:, None] * SEL_BLOCK)
                 & (np.arange(n16)[None, :] < n_cmp))
    overlap_cat = np.concatenate([overlap_t, overlap_t], axis=1)
    tri_t = np.arange(TK)[None, :] > np.arange(TK)[:, None]
    tri_cat = np.concatenate([tri_t, tri_t], axis=1)
    onehot = (np.arange(t_len)[:, None] // SEL_BLOCK) == sel[None, :]
    return (jnp.asarray(overlap_cat, BF16), jnp.asarray(tri_cat, BF16), jnp.asarray(onehot, BF16))


def _split_in_proj(w):
    sizes = (S5_CH, 512, 512, 512, 1024, 256, 256, 256, 256, 256, 256, 3 * NSA_HEADS)
    offs = np.cumsum((0,) + sizes)
    u5, sbq, sbk, sbv, nq, kc, vc, ksl, vsl, ksw, vsw, gt = (
        w[:, int(offs[n]):int(offs[n + 1])] for n in range(len(sizes)))
    w_nat = jnp.concatenate([sbk, kc, vc, ksl, ksw], axis=1).astype(BF16)
    w_q = jnp.concatenate([sbq, nq], axis=1).T.astype(BF16)
    w_v = jnp.concatenate([sbv, vsl, vsw], axis=1).T.astype(BF16)
    return u5.astype(BF16), w_nat, w_q, w_v, _gate_weights_t(gt)


def kernel(x, mem, w_in, w_out, s5_lambda_re, s5_lambda_im, s5_log_dt, s5_b_re, s5_b_im, s5_c_re, s5_c_im, s5_d, s5_w_glu, s5_b_glu, nsa_cmp_pos, nsa_cmp_w1, nsa_cmp_w2, rel_bias, xa_wq, xa_wkv, xa_wo, ffn_w_up, ffn_conv_w, ffn_conv_b, ffn_w_down, ln_g, ln_b):
    t_len = x.shape[1]
    overlap_cat, tri_cat, onehot = _static_tables(t_len)
    bias_win, bias_cmp = _bias_tables(rel_bias)
    mem_b = mem[0].astype(BF16)
    hf = x[0]
    hb = hf.astype(BF16)
    for l in range(DEPTH):
        w_u5, w_nat, w_q, w_v, w_gate_t = _split_in_proj(w_in[l])
        u5 = _mm(hb, w_u5, F32, 512, 512)
        proj = _mm_heads(hb, w_nat, 512)
        q_t = _mm_nt(w_q, hb, BF16, 512, TQ)
        v_t = _mm_nt(w_v, hb, BF16, 512, TK)
        gate_t = _mm_nt(w_gate_t, hb, F32, HEAD_DIM, TQ)

        s5p = _s5_params(s5_lambda_re[l], s5_lambda_im[l], s5_log_dt[l], s5_b_re[l], s5_b_im[l],
                         s5_c_re[l], s5_c_im[l])
        y_s5 = _s5(u5, s5p, s5_w_glu[l], s5_b_glu[l], s5_d[l])
        y_sb = _stick_breaking(proj, q_t, v_t, tri_cat)

        kvc, kvc_t = _compress(proj, HN_KC, nsa_cmp_pos[l], nsa_cmp_w1[l], nsa_cmp_w2[l])
        o_cmp_t, mask_t = _cmp_select(q_t, kvc, kvc_t, bias_cmp, rel_bias, overlap_cat)
        y_nsa = _slc_swa(proj, q_t, v_t, mask_t, _key_augment(proj, onehot), bias_win, rel_bias,
                         gate_t, o_cmp_t)

        y = jnp.concatenate([y_s5, y_sb, y_nsa], axis=1)
        hf, hb = _mm_ln(y, w_out[l].astype(BF16), hf, ln_g[l, 0], ln_b[l, 0], 512, 1024)

        kv = _mm(mem_b, xa_wkv[l].astype(BF16), BF16, N_MEM, 512)
        hf, hb = _xattn(hb, hf, xa_wq[l].astype(BF16), kv, xa_wo[l].astype(BF16),
                        ln_g[l, 1], ln_b[l, 1], 256)

        act = _ffn_up(hb, ffn_w_up[l], ffn_conv_w[l], ffn_conv_b[l], 1024, 512)
        hf, hb = _mm_ln(act, ffn_w_down[l].astype(BF16), hf, ln_g[l, 2], ln_b[l, 2], 512, 512)
    return hf[None]
```
